```python
import math
import jax, jax.numpy as jnp
from jax import lax
import numpy as np

D_MODEL = 2048
BATCH = 2
SEQ = 8192
DEPTH = 4

CHUNK = 64
Q_BLOCK = 128
EPS = 1e-6
NEG_INF = -1e30

FOX_HEADS = 8
FOX_HEAD_DIM = D_MODEL // 16
FOX_WIDTH = FOX_HEADS * FOX_HEAD_DIM
SSM_WIDTH = D_MODEL // 2
SSM_HEAD_DIM = 64
SSM_HEADS = SSM_WIDTH // SSM_HEAD_DIM
SSM_GROUPS = 2
SSM_STATE = 128
SSM_CONV = 4
SSM_CONV_DIM = SSM_WIDTH + 2 * SSM_GROUPS * SSM_STATE
SC_WIDTH = D_MODEL // 2
SC_CONV = 3
CF_WIDTH = D_MODEL // 2
CF_CONV = 31
N_BRANCH = 4
BRANCH_WIDTH = D_MODEL // 2

IN_SIZES = (
    FOX_WIDTH, FOX_WIDTH, FOX_WIDTH, FOX_HEADS, FOX_WIDTH,
    SSM_WIDTH, SSM_CONV_DIM, SSM_HEADS,
    SC_WIDTH, SC_WIDTH, SC_WIDTH, SC_WIDTH,
    2 * CF_WIDTH, CF_WIDTH,
)
N_IN = sum(IN_SIZES)

kernel_name = "hybrid_fox_ssd_shortconv_conformer_block"


def _split_columns(u, sizes):
    parts, off = [], 0
    for s in sizes:
        parts.append(u[..., off:off + s])
        off += s
    return parts


def rms_norm(x, w):
    xf = x.astype(jnp.float32)
    y = xf * lax.rsqrt(jnp.mean(xf * xf, axis=-1, keepdims=True) + EPS)
    return (y * w.astype(jnp.float32)).astype(x.dtype)


def layer_norm(x, w, b):
    xf = x.astype(jnp.float32)
    mu = jnp.mean(xf, axis=-1, keepdims=True)
    var = jnp.mean(jnp.square(xf - mu), axis=-1, keepdims=True)
    return ((xf - mu) * lax.rsqrt(var + EPS) * w + b).astype(x.dtype)


def causal_dwconv(x, w, b):
    k = w.shape[0]
    y = lax.conv_general_dilated(
        x, w[:, None, :].astype(x.dtype), window_strides=(1,), padding=((k - 1, 0),),
        dimension_numbers=("NWC", "WIO", "NWC"), feature_group_count=x.shape[-1])
    return y + b.astype(x.dtype)


def forgetting_attention(q, k, v, logf):
    bsz, s_len, n_h, dh = q.shape
    nb = s_len // Q_BLOCK
    scale = dh ** -0.5
    c = jnp.cumsum(logf, axis=1)
    qb = q.reshape(bsz, nb, Q_BLOCK, n_h, dh).transpose(1, 0, 3, 2, 4)
    cq = c.reshape(bsz, nb, Q_BLOCK, n_h).transpose(1, 0, 3, 2)
    kt = k.transpose(0, 2, 1, 3)
    vt = v.transpose(0, 2, 1, 3)
    ck = c.transpose(0, 2, 1)
    kpos = jnp.arange(s_len)

    def block(args):
        q_i, c_i, i = args
        s = jnp.einsum("bhqd,bhkd->bhqk", q_i, kt,
                       preferred_element_type=jnp.float32) * scale
        s = s + c_i[..., None] - ck[:, :, None, :]
        qpos = i * Q_BLOCK + jnp.arange(Q_BLOCK)
        s = jnp.where(kpos[None, :] <= qpos[:, None], s, NEG_INF)
        p = jax.nn.softmax(s, axis=-1)
        return jnp.einsum("bhqk,bhkd->bhqd", p.astype(vt.dtype), vt)

    o = lax.map(block, (qb, cq, jnp.arange(nb)))
    return o.transpose(1, 0, 3, 2, 4).reshape(bsz, s_len, n_h * dh)


def ssd(xh, dt, a, bm, cm, d_skip):
    bsz, s_len, n_h, p_dim = xh.shape
    g, n = bm.shape[2], bm.shape[3]
    r = n_h // g
    nc = s_len // CHUNK
    x = (xh * dt[..., None]).reshape(bsz, nc, CHUNK, g, r, p_dim)
    da = (dt * a).reshape(bsz, nc, CHUNK, g, r)
    bc = bm.reshape(bsz, nc, CHUNK, g, n)
    cc = cm.reshape(bsz, nc, CHUNK, g, n)
    cs = jnp.cumsum(da, axis=2)
    seg = cs[:, :, :, None] - cs[:, :, None, :]
    causal = jnp.tril(jnp.ones((CHUNK, CHUNK), dtype=bool))
    lmat = jnp.exp(jnp.where(causal[:, :, None, None], seg, -jnp.inf))
    cb = jnp.einsum("bclgn,bcsgn->bclsg", cc, bc)
    y_diag = jnp.einsum("bclsg,bclsgr,bcsgrp->bclgrp", cb, lmat, x)
    decay = jnp.exp(cs[:, :, -1:] - cs)
    states = jnp.einsum("bclgn,bclgr,bclgrp->bcgrpn", bc, decay, x)
    chunk_decay = jnp.exp(cs[:, :, -1])

    def step(h, inp):
        s_c, a_c = inp
        return h * a_c[..., None, None] + s_c, h

    h0 = jnp.zeros((bsz, g, r, p_dim, n), dtype=states.dtype)
    _, prev = lax.scan(step, h0, (jnp.moveaxis(states, 1, 0),
                                  jnp.moveaxis(chunk_decay, 1, 0).astype(states.dtype)))
    prev = jnp.moveaxis(prev, 0, 1)
    y_off = jnp.einsum("bclgn,bcgrpn,bclgr->bclgrp", cc, prev, jnp.exp(cs))
    y = (y_diag + y_off).reshape(bsz, s_len, n_h, p_dim) + xh * d_skip[:, None]
    return y.reshape(bsz, s_len, n_h * p_dim).astype(xh.dtype)


def hybrid_layer(x, norm_w, w_in, fg_bias, ssm_conv_w, ssm_conv_b, dt_bias, a_log, d_skip,
                 ssm_norm_w, sc_conv_w, sc_conv_b, cf_conv_w, cf_conv_b, cf_ln_w, cf_ln_b,
                 w_gate, b_gate, w_branch, w_out):
    bsz, s_len, _ = x.shape
    h = rms_norm(x, norm_w)
    u = h @ w_in
    (q, k, v, f_raw, g_a, z, xbc, dt_raw,
     sc_b, sc_c, sc_x, g_c, glu, g_d) = _split_columns(u, IN_SIZES)

    logf = jax.nn.log_sigmoid((f_raw + fg_bias).astype(jnp.float32))
    heads = lambda t: t.reshape(bsz, s_len, FOX_HEADS, FOX_HEAD_DIM)
    y_a = forgetting_attention(heads(q), heads(k), heads(v), logf) * jax.nn.silu(g_a)

    xbc = jax.nn.silu(causal_dwconv(xbc, ssm_conv_w, ssm_conv_b))
    xs, bm, cm = _split_columns(xbc, (SSM_WIDTH, SSM_GROUPS * SSM_STATE, SSM_GROUPS * SSM_STATE))
    dt = jax.nn.softplus((dt_raw + dt_bias).astype(jnp.float32))
    a = -jnp.exp(a_log.astype(jnp.float32))
    y = ssd(xs.reshape(bsz, s_len, SSM_HEADS, SSM_HEAD_DIM), dt, a,
            bm.reshape(bsz, s_len, SSM_GROUPS, SSM_STATE),
            cm.reshape(bsz, s_len, SSM_GROUPS, SSM_STATE), d_skip)
    y_b = rms_norm(y * jax.nn.silu(z), ssm_norm_w)

    y_c = sc_b * causal_dwconv(sc_c * sc_x, sc_conv_w, sc_conv_b) * jax.nn.silu(g_c)

    glu_a, glu_g = _split_columns(glu, (CF_WIDTH, CF_WIDTH))
    cf = causal_dwconv(glu_a * jax.nn.sigmoid(glu_g), cf_conv_w, cf_conv_b)
    y_d = jax.nn.silu(layer_norm(cf, cf_ln_w, cf_ln_b)) * jax.nn.silu(g_d)

    merged = None
    for i, y_i in enumerate((y_a, y_b, y_c, y_d)):
        gate = jax.nn.sigmoid(h @ w_gate[i] + b_gate[i])
        term = gate * (y_i @ w_branch[i])
        merged = term if merged is None else merged + term
    return x + merged @ w_out


def setup_inputs(seed: int = 0) -> dict:
    key = jax.random.key(seed)
    ks = jax.random.split(key, 24)
    f32 = jnp.float32
    L = DEPTH

    def nrm(k, shape, scale):
        return jax.random.normal(k, shape, f32) * scale

    dt0 = jnp.exp(jax.random.uniform(ks[6], (L, SSM_HEADS), f32, math.log(1e-3), math.log(1e-1)))
    return {
        "x": nrm(ks[0], (BATCH, SEQ, D_MODEL), 1.0),
        "norm_w": 1.0 + nrm(ks[1], (L, D_MODEL), 0.02),
        "w_in": nrm(ks[2], (L, D_MODEL, N_IN), D_MODEL ** -0.5),
        "fg_bias": jax.random.uniform(ks[3], (L, FOX_HEADS), f32, 1.0, 6.0),
        "ssm_conv_w": nrm(ks[4], (L, SSM_CONV, SSM_CONV_DIM), SSM_CONV ** -0.5),
        "ssm_conv_b": nrm(ks[5], (L, SSM_CONV_DIM), 0.02),
        "dt_bias": dt0 + jnp.log(-jnp.expm1(-dt0)),
        "a_log": jnp.log(jax.random.uniform(ks[7], (L, SSM_HEADS), f32, 1.0, 16.0)),
        "d_skip": 1.0 + nrm(ks[8], (L, SSM_HEADS), 0.02),
        "ssm_norm_w": 1.0 + nrm(ks[9], (L, SSM_WIDTH), 0.02),
        "sc_conv_w": nrm(ks[10], (L, SC_CONV, SC_WIDTH), SC_CONV ** -0.5),
        "sc_conv_b": nrm(ks[11], (L, SC_WIDTH), 0.02),
        "cf_conv_w": nrm(ks[12], (L, CF_CONV, CF_WIDTH), CF_CONV ** -0.5),
        "cf_conv_b": nrm(ks[13], (L, CF_WIDTH), 0.02),
        "cf_ln_w": 1.0 + nrm(ks[14], (L, CF_WIDTH), 0.02),
        "cf_ln_b": nrm(ks[15], (L, CF_WIDTH), 0.02),
        "w_gate": nrm(ks[16], (L, N_BRANCH, D_MODEL, D_MODEL), D_MODEL ** -0.5),
        "b_gate": nrm(ks[17], (L, N_BRANCH, D_MODEL), 0.02),
        "w_branch": nrm(ks[18], (L, N_BRANCH, BRANCH_WIDTH, D_MODEL), BRANCH_WIDTH ** -0.5),
        "w_out": nrm(ks[19], (L, D_MODEL, D_MODEL), D_MODEL ** -0.5),
        "final_norm_w": 1.0 + nrm(ks[20], (D_MODEL,), 0.02),
    }


def reference(x, norm_w, w_in, fg_bias, ssm_conv_w, ssm_conv_b, dt_bias, a_log, d_skip,
              ssm_norm_w, sc_conv_w, sc_conv_b, cf_conv_w, cf_conv_b, cf_ln_w, cf_ln_b,
              w_gate, b_gate, w_branch, w_out, final_norm_w):
    for l in range(DEPTH):
        x = hybrid_layer(x, norm_w[l], w_in[l], fg_bias[l], ssm_conv_w[l], ssm_conv_b[l],
                         dt_bias[l], a_log[l], d_skip[l], ssm_norm_w[l], sc_conv_w[l],
                         sc_conv_b[l], cf_conv_w[l], cf_conv_b[l], cf_ln_w[l], cf_ln_b[l],
                         w_gate[l], b_gate[l], w_branch[l], w_out[l])
    return rms_norm(x, final_norm_w)
```

```python
import functools

import jax
import jax.numpy as jnp
from jax import lax
from jax.experimental import pallas as pl
from jax.experimental.pallas import tpu as pltpu

F32 = jnp.float32
BF16 = jnp.bfloat16

D_MODEL = 2048
EPS = 1e-6
NEG_INF = -1e30

FOX_HEADS = 8
FOX_HEAD_DIM = 128
FOX_AUG = 2 * FOX_HEAD_DIM
SSM_HEADS = 16
SSM_HEAD_DIM = 64
SSM_GROUPS = 2
SSM_STATE = 128
SSM_CONV = 4
SC_CONV = 3
CF_CONV = 31
WIDTH = 1024
N_BRANCH = 4

COL_Q, COL_K, COL_V, COL_GA, COL_Z = 0, 1, 2, 3, 4
COL_SCB, COL_SCC, COL_SCX, COL_GC = 5, 6, 7, 8
COL_GLUA, COL_GLUG, COL_GD, COL_XS = 9, 10, 11, 12
COL_BC_512 = 26
N_MAIN = 13 * WIDTH + 512
LANE_F = 0
LANE_DT = 8
SMALL = 128

HALO = 8
CF_HALO = 32

V7X_VMEM_LIMIT = 56 * 1024 * 1024


def _cparams(sem):
    return pltpu.CompilerParams(dimension_semantics=sem, vmem_limit_bytes=V7X_VMEM_LIMIT)


def _sigmoid(x):
    return 1.0 / (1.0 + jnp.exp(-x))


def _silu(x):
    return x * _sigmoid(x)


def _softplus(x):
    return jnp.maximum(x, 0.0) + jnp.log1p(jnp.exp(-jnp.abs(x)))


def _split3(x):
    hi = x.astype(BF16)
    r1 = x - hi.astype(F32)
    mid = r1.astype(BF16)
    lo = (r1 - mid.astype(F32)).astype(BF16)
    return hi, mid, lo


def _exact_dot(m_bf16, x):
    hi, mid, lo = _split3(x)
    d = lambda p: jnp.dot(m_bf16, p, preferred_element_type=F32)
    return d(hi) + d(mid) + d(lo)


def _exact_dot_r(x, m_bf16):
    hi, mid, lo = _split3(x)
    d = lambda p: jnp.dot(p, m_bf16, preferred_element_type=F32)
    return d(hi) + d(mid) + d(lo)


def _tril_ones(n):
    r = lax.broadcasted_iota(jnp.int32, (n, n), 0)
    c = lax.broadcasted_iota(jnp.int32, (n, n), 1)
    return r >= c


def _rmsnorm_kernel(x_ref, w_ref, h_ref):
    x = x_ref[...]
    y = x * lax.rsqrt(jnp.mean(x * x, axis=-1, keepdims=True) + EPS)
    h_ref[...] = (y * w_ref[...]).astype(h_ref.dtype)


def _rmsnorm(x2d, w, tm):
    m, d = x2d.shape
    return pl.pallas_call(
        _rmsnorm_kernel,
        grid=(m // tm,),
        in_specs=[pl.BlockSpec((tm, d), lambda i: (i, 0)),
                  pl.BlockSpec((1, d), lambda i: (0, 0))],
        out_specs=pl.BlockSpec((tm, d), lambda i: (i, 0)),
        out_shape=jax.ShapeDtypeStruct((m, d), BF16),
        compiler_params=_cparams(("parallel",)),
        name="rmsnorm",
    )(x2d, w.reshape(1, d))


def _matmul_kernel(h_ref, w_ref, o_ref):
    o_ref[...] = jnp.dot(h_ref[...], w_ref[...], preferred_element_type=F32).astype(o_ref.dtype)


def _matmul(h, w, tm, tn, out_dtype, name):
    m, k = h.shape
    n = w.shape[1]
    return pl.pallas_call(
        _matmul_kernel,
        grid=(m // tm, n // tn),
        in_specs=[pl.BlockSpec((tm, k), lambda i, j: (i, 0)),
                  pl.BlockSpec((k, tn), lambda i, j: (0, j))],
        out_specs=pl.BlockSpec((tm, tn), lambda i, j: (i, j)),
        out_shape=jax.ShapeDtypeStruct((m, n), out_dtype),
        compiler_params=_cparams(("parallel", "arbitrary")),
        name=name,
    )(h, w)


def _fox_prep_kernel(small_ref, bias_ref, q_ref, k_ref, qp_ref, kp_ref, carry_ref):
    ts = small_ref.shape[0]

    @pl.when(pl.program_id(1) == 0)
    def _():
        carry_ref[...] = jnp.zeros_like(carry_ref)

    f = small_ref[...] + bias_ref[...]
    logf = -_softplus(-f)
    tri = _tril_ones(ts).astype(BF16)
    c = _exact_dot(tri, logf) + carry_ref[0:1, :]
    carry_ref[...] = jnp.broadcast_to(c[ts - 1:ts, :], carry_ref.shape)

    lane = lax.broadcasted_iota(jnp.int32, (ts, FOX_HEAD_DIM), 1)
    for h in range(FOX_HEADS):
        ch = c[:, LANE_F + h:LANE_F + h + 1]
        hi = ch.astype(BF16).astype(F32)
        r1 = ch - hi
        mid = r1.astype(BF16).astype(F32)
        lo = (r1 - mid).astype(BF16).astype(F32)
        ones = jnp.where(lane < 6, 1.0, 0.0)
        aq = jnp.where(lane == 0, hi, jnp.where(lane == 1, mid, jnp.where(lane == 2, lo, ones)))
        ones_k = jnp.where(lane < 3, 1.0, 0.0)
        ak = jnp.where(lane == 3, -hi, jnp.where(lane == 4, -mid, jnp.where(lane == 5, -lo, ones_k)))
        sl = slice(h * FOX_HEAD_DIM, (h + 1) * FOX_HEAD_DIM)
        qp_ref[h, :, 0:FOX_HEAD_DIM] = q_ref[:, sl]
        qp_ref[h, :, FOX_HEAD_DIM:FOX_AUG] = aq.astype(BF16)
        kp_ref[h, :, 0:FOX_HEAD_DIM] = k_ref[:, sl]
        kp_ref[h, :, FOX_HEAD_DIM:FOX_AUG] = ak.astype(BF16)


def _fox_prep(u, small, bias_row, bsz, s_len, ts):
    nt = s_len // ts
    row = lambda b, i: b * nt + i
    out = jax.ShapeDtypeStruct((bsz, FOX_HEADS, s_len, FOX_AUG), BF16)
    ospec = pl.BlockSpec((None, FOX_HEADS, ts, FOX_AUG), lambda b, i: (b, 0, i, 0))
    return pl.pallas_call(
        _fox_prep_kernel,
        grid=(bsz, nt),
        in_specs=[pl.BlockSpec((ts, SMALL), lambda b, i: (row(b, i), 0)),
                  pl.BlockSpec((1, SMALL), lambda b, i: (0, 0)),
                  pl.BlockSpec((ts, WIDTH), lambda b, i: (row(b, i), COL_Q)),
                  pl.BlockSpec((ts, WIDTH), lambda b, i: (row(b, i), COL_K))],
        out_specs=[ospec, ospec],
        out_shape=[out, out],
        scratch_shapes=[pltpu.VMEM((8, SMALL), F32)],
        compiler_params=_cparams(("parallel", "arbitrary")),
        name="fox_prep",
    )(small, bias_row, u, u)


def _flash_kernel(q_ref, k_ref, v_ref, g_ref, o_ref, m_ref, l_ref, acc_ref):
    qi = pl.program_id(2)
    kj = pl.program_id(3)
    tq, tk = q_ref.shape[0], k_ref.shape[0]

    @pl.when(kj == 0)
    def _():
        m_ref[...] = jnp.full_like(m_ref, -jnp.inf)
        l_ref[...] = jnp.zeros_like(l_ref)
        acc_ref[...] = jnp.zeros_like(acc_ref)

    def step(masked):
        s = lax.dot_general(q_ref[...], k_ref[...], (((1,), (1,)), ((), ())),
                            preferred_element_type=F32)
        if masked:
            r = lax.broadcasted_iota(jnp.int32, (tq, tk), 0)
            c = lax.broadcasted_iota(jnp.int32, (tq, tk), 1)
            s = jnp.where(c <= r, s, NEG_INF)
        m_prev = m_ref[...]
        m_new = jnp.maximum(m_prev, jnp.max(s, axis=-1, keepdims=True))
        alpha = jnp.exp(m_prev - m_new)
        p = jnp.exp(s - m_new[:, 0:1])
        l_ref[...] = alpha * l_ref[...] + jnp.sum(p, axis=-1, keepdims=True)
        acc_ref[...] = alpha * acc_ref[...] + jnp.dot(p.astype(BF16), v_ref[...],
                                                      preferred_element_type=F32)
        m_ref[...] = m_new

    @pl.when(kj < qi)
    def _():
        step(False)

    @pl.when(kj == qi)
    def _():
        step(True)
        o = acc_ref[...] / l_ref[...]
        o_ref[...] = (o * _silu(g_ref[...].astype(F32))).astype(o_ref.dtype)


def _flash(qp, kp, u, bsz, s_len, tq):
    nq = s_len // tq
    kv = lambda qi, kj: jnp.minimum(kj, qi)
    return pl.pallas_call(
        _flash_kernel,
        grid=(bsz, FOX_HEADS, nq, nq),
        in_specs=[
            pl.BlockSpec((None, None, tq, FOX_AUG), lambda b, h, qi, kj: (b, h, qi, 0)),
            pl.BlockSpec((None, None, tq, FOX_AUG), lambda b, h, qi, kj: (b, h, kv(qi, kj), 0)),
            pl.BlockSpec((tq, FOX_HEAD_DIM),
                         lambda b, h, qi, kj: (b * nq + kv(qi, kj), COL_V * FOX_HEADS + h)),
            pl.BlockSpec((tq, FOX_HEAD_DIM),
                         lambda b, h, qi, kj: (b * nq + qi, COL_GA * FOX_HEADS + h)),
        ],
        out_specs=pl.BlockSpec((tq, FOX_HEAD_DIM), lambda b, h, qi, kj: (b * nq + qi, h)),
        out_shape=jax.ShapeDtypeStruct((bsz * s_len, WIDTH), BF16),
        scratch_shapes=[pltpu.VMEM((tq, FOX_HEAD_DIM), F32)] * 3,
        compiler_params=_cparams(("parallel", "parallel", "parallel", "arbitrary")),
        name="fox_flash",
    )(qp, kp, u, u)


def _causal_conv(ext_ref, w_ref, b_ref, taps, halo, t):
    acc = None
    for k in range(taps):
        off = halo - (taps - 1) + k
        term = w_ref[k:k + 1, :] * ext_ref[off:off + t, :]
        acc = term if acc is None else acc + term
    return acc + b_ref[...]


def _ssd_kernel(xs_ref, bc_ref, z_ref, small_ref, cwx_ref, cbx_ref, cwbc_ref, cbbc_ref,
                dtb_ref, alog_ref, dskip_ref, nw_ref, o_ref,
                extx_ref, extbc_ref, state_ref):
    t = xs_ref.shape[0]
    hp = SSM_HEADS // SSM_GROUPS * SSM_HEAD_DIM

    @pl.when(pl.program_id(1) == 0)
    def _():
        extx_ref[0:HALO, :] = jnp.zeros((HALO, WIDTH), F32)
        extbc_ref[0:HALO, :] = jnp.zeros((HALO, 512), F32)
        state_ref[...] = jnp.zeros_like(state_ref)

    extx_ref[HALO:HALO + t, :] = xs_ref[...].astype(F32)
    extbc_ref[HALO:HALO + t, :] = bc_ref[...].astype(F32)
    xs = _silu(_causal_conv(extx_ref, cwx_ref, cbx_ref, SSM_CONV, HALO, t))
    bc = _silu(_causal_conv(extbc_ref, cwbc_ref, cbbc_ref, SSM_CONV, HALO, t))
    extx_ref[0:HALO, :] = extx_ref[t:t + HALO, :]
    extbc_ref[0:HALO, :] = extbc_ref[t:t + HALO, :]

    dt = _softplus(small_ref[...] + dtb_ref[...])
    da = dt * (-jnp.exp(alog_ref[...]))
    tri = _tril_ones(t)
    cs = _exact_dot(tri.astype(BF16), da)
    er = lax.broadcasted_iota(jnp.int32, (SMALL, WIDTH), 0)
    ec = lax.broadcasted_iota(jnp.int32, (SMALL, WIDTH), 1)
    expand = (er == LANE_DT + ec // SSM_HEAD_DIM).astype(BF16)
    dt_e = _exact_dot_r(dt, expand)
    cs_e = _exact_dot_r(cs, expand)
    cs_last_e = cs_e[t - 1:t, :]

    xdt = xs * dt_e
    xdt_b = xdt.astype(BF16)
    xdec_b = (xdt * jnp.exp(cs_last_e - cs_e)).astype(BF16)
    cs_t = cs.T
    lane = lax.broadcasted_iota(jnp.int32, (t, 2 * SSM_HEAD_DIM), 1)
    lo_half = lane < SSM_HEAD_DIM

    y_parts = []
    for g in range(SSM_GROUPS):
        b_g = bc[:, g * SSM_STATE:(g + 1) * SSM_STATE].astype(BF16)
        c_g = bc[:, (SSM_GROUPS + g) * SSM_STATE:(SSM_GROUPS + g + 1) * SSM_STATE].astype(BF16)
        cb = lax.dot_general(c_g, b_g, (((1,), (1,)), ((), ())), preferred_element_type=F32)
        for pair in range(SSM_HEADS // SSM_GROUPS // 2):
            h0 = g * (SSM_HEADS // SSM_GROUPS) + 2 * pair
            gs = []
            for h in (h0, h0 + 1):
                seg = cs[:, LANE_DT + h:LANE_DT + h + 1] - cs_t[LANE_DT + h:LANE_DT + h + 1, :]
                lmat = jnp.exp(jnp.where(tri, seg, -jnp.inf))
                gs.append((cb * lmat).astype(BF16))
            xp = xdt_b[:, h0 * SSM_HEAD_DIM:(h0 + 2) * SSM_HEAD_DIM]
            zero = jnp.zeros_like(xp)
            rhs = jnp.concatenate([jnp.where(lo_half, xp, zero), jnp.where(lo_half, zero, xp)], axis=0)
            y_parts.append(jnp.dot(jnp.concatenate(gs, axis=1), rhs, preferred_element_type=F32))
    y_diag = jnp.concatenate(y_parts, axis=1)

    y_off_parts = []
    for g in range(SSM_GROUPS):
        b_g = bc[:, g * SSM_STATE:(g + 1) * SSM_STATE].astype(BF16)
        c_g = bc[:, (SSM_GROUPS + g) * SSM_STATE:(SSM_GROUPS + g + 1) * SSM_STATE].astype(BF16)
        prev = state_ref[g]
        y_off_parts.append(jnp.dot(c_g, prev.astype(BF16), preferred_element_type=F32))
        upd = lax.dot_general(b_g, xdec_b[:, g * hp:(g + 1) * hp], (((0,), (0,)), ((), ())),
                              preferred_element_type=F32)
        state_ref[g] = prev * jnp.exp(cs_last_e[:, g * hp:(g + 1) * hp]) + upd
    y_off = jnp.concatenate(y_off_parts, axis=1) * jnp.exp(cs_e)

    y = y_diag + y_off + xs * dskip_ref[...]
    v = y * _silu(z_ref[...].astype(F32))
    v = v * lax.rsqrt(jnp.mean(v * v, axis=-1, keepdims=True) + EPS)
    o_ref[...] = (v * nw_ref[...]).astype(o_ref.dtype)


def _ssd(u, small, p, bsz, s_len, t):
    nt = s_len // t
    row = lambda b, i: b * nt + i
    full = lambda shape: pl.BlockSpec(shape, lambda b, i: (0, 0))
    return pl.pallas_call(
        _ssd_kernel,
        grid=(bsz, nt),
        in_specs=[pl.BlockSpec((t, WIDTH), lambda b, i: (row(b, i), COL_XS)),
                  pl.BlockSpec((t, 512), lambda b, i: (row(b, i), COL_BC_512)),
                  pl.BlockSpec((t, WIDTH), lambda b, i: (row(b, i), COL_Z)),
                  pl.BlockSpec((t, SMALL), lambda b, i: (row(b, i), 0)),
                  full((SSM_CONV, WIDTH)), full((1, WIDTH)),
                  full((SSM_CONV, 512)), full((1, 512)),
                  full((1, SMALL)), full((1, SMALL)), full((1, WIDTH)), full((1, WIDTH))],
        out_specs=pl.BlockSpec((t, WIDTH), lambda b, i: (row(b, i), 0)),
        out_shape=jax.ShapeDtypeStruct((bsz * s_len, WIDTH), BF16),
        scratch_shapes=[pltpu.VMEM((HALO + t, WIDTH), F32),
                        pltpu.VMEM((HALO + t, 512), F32),
                        pltpu.VMEM((SSM_GROUPS, SSM_STATE, 512), F32)],
        compiler_params=_cparams(("parallel", "arbitrary")),
        name="ssd",
    )(u, u, u, small, p["cwx"], p["cbx"], p["cwbc"], p["cbbc"],
      p["dtb"], p["alog"], p["dskip"], p["ssm_nw"])


def _shortconv_kernel(b_ref, c_ref, x_ref, g_ref, w_ref, cb_ref, o_ref, ext_ref):
    t = x_ref.shape[0]

    @pl.when(pl.program_id(1) == 0)
    def _():
        ext_ref[0:HALO, :] = jnp.zeros((HALO, WIDTH), F32)

    ext_ref[HALO:HALO + t, :] = c_ref[...].astype(F32) * x_ref[...].astype(F32)
    conv = _causal_conv(ext_ref, w_ref, cb_ref, SC_CONV, HALO, t)
    ext_ref[0:HALO, :] = ext_ref[t:t + HALO, :]
    o_ref[...] = (b_ref[...].astype(F32) * conv * _silu(g_ref[...].astype(F32))).astype(o_ref.dtype)


def _shortconv(u, p, bsz, s_len, t):
    nt = s_len // t
    row = lambda b, i: b * nt + i
    col = lambda c: pl.BlockSpec((t, WIDTH), lambda b, i: (row(b, i), c))
    full = lambda shape: pl.BlockSpec(shape, lambda b, i: (0, 0))
    return pl.pallas_call(
        _shortconv_kernel,
        grid=(bsz, nt),
        in_specs=[col(COL_SCB), col(COL_SCC), col(COL_SCX), col(COL_GC),
                  full((SC_CONV, WIDTH)), full((1, WIDTH))],
        out_specs=pl.BlockSpec((t, WIDTH), lambda b, i: (row(b, i), 0)),
        out_shape=jax.ShapeDtypeStruct((bsz * s_len, WIDTH), BF16),
        scratch_shapes=[pltpu.VMEM((HALO + t, WIDTH), F32)],
        compiler_params=_cparams(("parallel", "arbitrary")),
        name="shortconv",
    )(u, u, u, u, p["sc_w"], p["sc_b"])


def _conformer_kernel(a_ref, gg_ref, g_ref, w_ref, cb_ref, lnw_ref, lnb_ref, o_ref, ext_ref):
    t = a_ref.shape[0]

    @pl.when(pl.program_id(1) == 0)
    def _():
        ext_ref[0:CF_HALO, :] = jnp.zeros((CF_HALO, WIDTH), F32)

    ext_ref[CF_HALO:CF_HALO + t, :] = a_ref[...].astype(F32) * _sigmoid(gg_ref[...].astype(F32))
    cf = _causal_conv(ext_ref, w_ref, cb_ref, CF_CONV, CF_HALO, t)
    ext_ref[0:CF_HALO, :] = ext_ref[t:t + CF_HALO, :]
    mu = jnp.mean(cf, axis=-1, keepdims=True)
    d = cf - mu
    var = jnp.mean(d * d, axis=-1, keepdims=True)
    ln = d * lax.rsqrt(var + EPS) * lnw_ref[...] + lnb_ref[...]
    o_ref[...] = (_silu(ln) * _silu(g_ref[...].astype(F32))).astype(o_ref.dtype)


def _conformer(u, p, bsz, s_len, t):
    nt = s_len // t
    row = lambda b, i: b * nt + i
    col = lambda c: pl.BlockSpec((t, WIDTH), lambda b, i: (row(b, i), c))
    full = lambda shape: pl.BlockSpec(shape, lambda b, i: (0, 0))
    return pl.pallas_call(
        _conformer_kernel,
        grid=(bsz, nt),
        in_specs=[col(COL_GLUA), col(COL_GLUG), col(COL_GD),
                  full((CF_CONV, WIDTH)), full((1, WIDTH)), full((1, WIDTH)), full((1, WIDTH))],
        out_specs=pl.BlockSpec((t, WIDTH), lambda b, i: (row(b, i), 0)),
        out_shape=jax.ShapeDtypeStruct((bsz * s_len, WIDTH), BF16),
        scratch_shapes=[pltpu.VMEM((CF_HALO + t, WIDTH), F32)],
        compiler_params=_cparams(("parallel", "arbitrary")),
        name="conformer",
    )(u, u, u, p["cf_w"], p["cf_b"], p["cf_lnw"], p["cf_lnb"])


def _merge_kernel(h_ref, ya_ref, yb_ref, yc_ref, yd_ref, wg_ref, bg_ref, wb_ref, o_ref):
    h = h_ref[...]
    acc = None
    for i, y_ref in enumerate((ya_ref, yb_ref, yc_ref, yd_ref)):
        gate = _sigmoid(jnp.dot(h, wg_ref[i], preferred_element_type=F32) + bg_ref[i:i + 1, :])
        term = gate * jnp.dot(y_ref[...], wb_ref[i], preferred_element_type=F32)
        acc = term if acc is None else acc + term
    o_ref[...] = acc.astype(o_ref.dtype)


def _merge(h, ys, wg, bg, wb, tm, tn):
    m = h.shape[0]
    yspec = pl.BlockSpec((tm, WIDTH), lambda i, j: (i, 0))
    return pl.pallas_call(
        _merge_kernel,
        grid=(m // tm, D_MODEL // tn),
        in_specs=[pl.BlockSpec((tm, D_MODEL), lambda i, j: (i, 0)),
                  yspec, yspec, yspec, yspec,
                  pl.BlockSpec((N_BRANCH, D_MODEL, tn), lambda i, j: (0, 0, j)),
                  pl.BlockSpec((N_BRANCH, tn), lambda i, j: (0, j)),
                  pl.BlockSpec((N_BRANCH, WIDTH, tn), lambda i, j: (0, 0, j))],
        out_specs=pl.BlockSpec((tm, tn), lambda i, j: (i, j)),
        out_shape=jax.ShapeDtypeStruct((m, D_MODEL), BF16),
        compiler_params=_cparams(("parallel", "arbitrary")),
        name="merge",
    )(h, *ys, wg, bg, wb)


def _out_kernel(x_ref, mg_ref, w_ref, nw_ref, *out_refs, last):
    x_new = x_ref[...] + jnp.dot(mg_ref[...], w_ref[...], preferred_element_type=F32)
    normed = x_new * lax.rsqrt(jnp.mean(x_new * x_new, axis=-1, keepdims=True) + EPS) * nw_ref[...]
    if last:
        out_refs[0][...] = normed.astype(out_refs[0].dtype)
    else:
        out_refs[0][...] = x_new
        out_refs[1][...] = normed.astype(out_refs[1].dtype)


def _out_proj(x2d, merged, w_out, next_norm_w, tm, last):
    m, d = x2d.shape
    row = pl.BlockSpec((tm, d), lambda i: (i, 0))
    if last:
        out_shape = [jax.ShapeDtypeStruct((m, d), F32)]
    else:
        out_shape = [jax.ShapeDtypeStruct((m, d), F32), jax.ShapeDtypeStruct((m, d), BF16)]
    return pl.pallas_call(
        functools.partial(_out_kernel, last=last),
        grid=(m // tm,),
        in_specs=[row, row,
                  pl.BlockSpec((d, d), lambda i: (0, 0)),
                  pl.BlockSpec((1, d), lambda i: (0, 0))],
        out_specs=[row] * len(out_shape),
        out_shape=out_shape,
        compiler_params=_cparams(("parallel",)),
        name="out_proj_last" if last else "out_proj",
    )(x2d, merged, w_out, next_norm_w.reshape(1, d))


def _pad_lanes(v, start):
    n = v.shape[-1]
    return jnp.pad(v.astype(F32), ((0, 0), (start, SMALL - start - n)))[:, None, :]


def _prep_w_in(w_in):
    scale = FOX_HEAD_DIM ** -0.5
    o = 0
    parts = {}
    for name, size in (("q", 1024), ("k", 1024), ("v", 1024), ("f", FOX_HEADS), ("ga", 1024),
                       ("z", 1024), ("xbc", 1536), ("dt", SSM_HEADS), ("scb", 1024), ("scc", 1024),
                       ("scx", 1024), ("gc", 1024), ("glu", 2048), ("gd", 1024)):
        parts[name] = w_in[:, :, o:o + size]
        o += size
    main = jnp.concatenate(
        [parts["q"] * scale, parts["k"], parts["v"], parts["ga"], parts["z"],
         parts["scb"], parts["scc"], parts["scx"], parts["gc"], parts["glu"], parts["gd"],
         parts["xbc"]], axis=-1).astype(BF16)
    pad = jnp.zeros(w_in.shape[:2] + (SMALL - FOX_HEADS - SSM_HEADS,), w_in.dtype)
    small = jnp.concatenate([parts["f"], parts["dt"], pad], axis=-1).astype(BF16)
    return main, small


def _tiles(m, s_len):
    pick = lambda want, n: min(want, n)
    return dict(tm_in=pick(1024, m), tn_in=1536, tm_norm=pick(512, m), t_prep=pick(256, s_len),
                tq=pick(512, s_len), t_ssd=pick(128, s_len), t_conv=pick(256, s_len),
                tm_merge=pick(512, m), tn_merge=512, tm_out=pick(512, m))


def kernel(x, norm_w, w_in, fg_bias, ssm_conv_w, ssm_conv_b, dt_bias, a_log, d_skip, ssm_norm_w,
           sc_conv_w, sc_conv_b, cf_conv_w, cf_conv_b, cf_ln_w, cf_ln_b, w_gate, b_gate, w_branch,
           w_out, final_norm_w):
    bsz, s_len, d = x.shape
    depth = w_in.shape[0]
    m = bsz * s_len
    tl = _tiles(m, s_len)

    w_main, w_small = _prep_w_in(w_in)
    wg = w_gate.astype(BF16)
    wb = w_branch.astype(BF16)
    wo = w_out.astype(BF16)
    fg_row = _pad_lanes(fg_bias, LANE_F)
    dtb_row = _pad_lanes(dt_bias, LANE_DT)
    alog_row = _pad_lanes(a_log, LANE_DT)
    dskip_row = jnp.repeat(d_skip.astype(F32), SSM_HEAD_DIM, axis=-1)[:, None, :]
    row = lambda a: a.astype(F32)[:, None, :]

    x2d = x.reshape(m, d)
    h = _rmsnorm(x2d, norm_w[0], tl["tm_norm"])
    for l in range(depth):
        u = _matmul(h, w_main[l], tl["tm_in"], tl["tn_in"], BF16, "in_proj")
        small = _matmul(h, w_small[l], tl["tm_in"], SMALL, F32, "in_proj_small")
        qp, kp = _fox_prep(u, small, fg_row[l], bsz, s_len, tl["t_prep"])
        y_a = _flash(qp, kp, u, bsz, s_len, tl["tq"])
        ssd_p = dict(cwx=ssm_conv_w[l, :, :WIDTH], cbx=ssm_conv_b[l, None, :WIDTH],
                     cwbc=ssm_conv_w[l, :, WIDTH:], cbbc=ssm_conv_b[l, None, WIDTH:],
                     dtb=dtb_row[l], alog=alog_row[l], dskip=dskip_row[l], ssm_nw=row(ssm_norm_w)[l])
        y_b = _ssd(u, small, ssd_p, bsz, s_len, tl["t_ssd"])
        y_c = _shortconv(u, dict(sc_w=sc_conv_w[l], sc_b=sc_conv_b[l, None, :]), bsz, s_len, tl["t_conv"])
        y_d = _conformer(u, dict(cf_w=cf_conv_w[l], cf_b=cf_conv_b[l, None, :],
                                 cf_lnw=cf_ln_w[l, None, :], cf_lnb=cf_ln_b[l, None, :]),
                         bsz, s_len, tl["t_conv"])
        merged = _merge(h, (y_a, y_b, y_c, y_d), wg[l], b_gate[l], wb[l], tl["tm_merge"], tl["tn_merge"])
        last = l == depth - 1
        nw = final_norm_w if last else norm_w[l + 1]
        outs = _out_proj(x2d, merged, wo[l], nw, tl["tm_out"], last)
        if last:
            return outs[0].reshape(bsz, s_len, d)
        x2d, h = outs
```

```python
import functools

import jax
import jax.numpy as jnp
from jax import lax
from jax.experimental import pallas as pl
from jax.experimental.pallas import tpu as pltpu

F32 = jnp.float32
BF16 = jnp.bfloat16

D_MODEL = 2048
EPS = 1e-6
NEG_INF = -1e30
LOG2E = 1.4426950408889634

FOX_HEADS = 8
FOX_HEAD_DIM = 128
FOX_AUG = 2 * FOX_HEAD_DIM
SSM_HEADS = 16
SSM_HEAD_DIM = 64
SSM_GROUPS = 2
SSM_STATE = 128
SSM_CONV = 4
SC_CONV = 3
CF_CONV = 31
WIDTH = 1024
N_BRANCH = 4

COL_Q, COL_K, COL_V, COL_GA, COL_Z = 0, 1, 2, 3, 4
COL_SCB, COL_SCC, COL_SCX, COL_GC = 5, 6, 7, 8
COL_GLUA, COL_GLUG, COL_GD, COL_XS = 9, 10, 11, 12
COL_BC_512 = 26
N_MAIN = 13 * WIDTH + 512
LANE_F = 0
LANE_DT = 8
SMALL = 128

HALO = 8
CF_HALO = 32

V7X_VMEM_LIMIT = 56 * 1024 * 1024


def _cparams(sem):
    return pltpu.CompilerParams(dimension_semantics=sem, vmem_limit_bytes=V7X_VMEM_LIMIT)


def _sigmoid(x):
    return 1.0 / (1.0 + jnp.exp(-x))


def _silu(x):
    return x * _sigmoid(x)


def _softplus(x):
    return jnp.maximum(x, 0.0) + jnp.log1p(jnp.exp(-jnp.abs(x)))


def _split3(x):
    hi = x.astype(BF16)
    r1 = x - hi.astype(F32)
    mid = r1.astype(BF16)
    lo = (r1 - mid.astype(F32)).astype(BF16)
    return hi, mid, lo


def _exact_dot(m_bf16, x):
    hi, mid, lo = _split3(x)
    d = lambda p: jnp.dot(m_bf16, p, preferred_element_type=F32)
    return d(hi) + d(mid) + d(lo)


def _exact_dot_r(x, m_bf16):
    hi, mid, lo = _split3(x)
    d = lambda p: jnp.dot(p, m_bf16, preferred_element_type=F32)
    return d(hi) + d(mid) + d(lo)


def _tril_ones(n):
    r = lax.broadcasted_iota(jnp.int32, (n, n), 0)
    c = lax.broadcasted_iota(jnp.int32, (n, n), 1)
    return r >= c


def _rmsnorm_kernel(x_ref, w_ref, h_ref):
    x = x_ref[...]
    y = x * lax.rsqrt(jnp.mean(x * x, axis=-1, keepdims=True) + EPS)
    h_ref[...] = (y * w_ref[...]).astype(h_ref.dtype)


def _rmsnorm(x2d, w, tm):
    m, d = x2d.shape
    return pl.pallas_call(
        _rmsnorm_kernel,
        grid=(m // tm,),
        in_specs=[pl.BlockSpec((tm, d), lambda i: (i, 0)),
                  pl.BlockSpec((1, d), lambda i: (0, 0))],
        out_specs=pl.BlockSpec((tm, d), lambda i: (i, 0)),
        out_shape=jax.ShapeDtypeStruct((m, d), BF16),
        compiler_params=_cparams(("parallel",)),
        name="rmsnorm",
    )(x2d, w.reshape(1, d))


def _matmul_kernel(h_ref, w_ref, o_ref):
    o_ref[...] = jnp.dot(h_ref[...], w_ref[...], preferred_element_type=F32).astype(o_ref.dtype)


def _matmul(h, w, tm, tn, out_dtype, name):
    m, k = h.shape
    n = w.shape[1]
    return pl.pallas_call(
        _matmul_kernel,
        grid=(m // tm, n // tn),
        in_specs=[pl.BlockSpec((tm, k), lambda i, j: (i, 0)),
                  pl.BlockSpec((k, tn), lambda i, j: (0, j))],
        out_specs=pl.BlockSpec((tm, tn), lambda i, j: (i, j)),
        out_shape=jax.ShapeDtypeStruct((m, n), out_dtype),
        compiler_params=_cparams(("parallel", "arbitrary")),
        name=name,
    )(h, w)


def _fox_prep_kernel(small_ref, bias_ref, q_ref, k_ref, v_ref, qp_ref, kp_ref, vt_ref, carry_ref):
    ts = small_ref.shape[0]

    @pl.when(pl.program_id(1) == 0)
    def _():
        carry_ref[...] = jnp.zeros_like(carry_ref)

    f = small_ref[...] + bias_ref[...]
    logf = -_softplus(-f)
    tri = _tril_ones(ts).astype(BF16)
    c = _exact_dot(tri, logf) + carry_ref[0:1, :]
    carry_ref[...] = jnp.broadcast_to(c[ts - 1:ts, :], carry_ref.shape)
    c = c * LOG2E

    lane = lax.broadcasted_iota(jnp.int32, (ts, FOX_HEAD_DIM), 1)
    ones_q = jnp.where(lane < 6, 1.0, 0.0)
    ones_k = jnp.where(lane < 3, 1.0, 0.0)
    for h in range(FOX_HEADS):
        ch = c[:, LANE_F + h:LANE_F + h + 1]
        hi = ch.astype(BF16).astype(F32)
        r1 = ch - hi
        mid = r1.astype(BF16).astype(F32)
        lo = (r1 - mid).astype(BF16).astype(F32)
        aq = jnp.where(lane == 0, hi, jnp.where(lane == 1, mid, jnp.where(lane == 2, lo, ones_q)))
        ak = jnp.where(lane == 3, -hi, jnp.where(lane == 4, -mid, jnp.where(lane == 5, -lo, ones_k)))
        sl = slice(h * FOX_HEAD_DIM, (h + 1) * FOX_HEAD_DIM)
        qp_ref[h, :, 0:FOX_HEAD_DIM] = q_ref[:, sl]
        qp_ref[h, :, FOX_HEAD_DIM:FOX_AUG] = aq.astype(BF16)
        kp_ref[h, :, 0:FOX_HEAD_DIM] = k_ref[:, sl]
        kp_ref[h, :, FOX_HEAD_DIM:FOX_AUG] = ak.astype(BF16)
        vt_ref[h] = v_ref[:, sl].astype(F32).T.astype(BF16)


def _fox_prep(u, small, bias_row, bsz, s_len, ts):
    nt = s_len // ts
    row = lambda b, i: b * nt + i
    out = jax.ShapeDtypeStruct((bsz, FOX_HEADS, s_len, FOX_AUG), BF16)
    ospec = pl.BlockSpec((None, FOX_HEADS, ts, FOX_AUG), lambda b, i: (b, 0, i, 0))
    return pl.pallas_call(
        _fox_prep_kernel,
        grid=(bsz, nt),
        in_specs=[pl.BlockSpec((ts, SMALL), lambda b, i: (row(b, i), 0)),
                  pl.BlockSpec((1, SMALL), lambda b, i: (0, 0)),
                  pl.BlockSpec((ts, WIDTH), lambda b, i: (row(b, i), COL_Q)),
                  pl.BlockSpec((ts, WIDTH), lambda b, i: (row(b, i), COL_K)),
                  pl.BlockSpec((ts, WIDTH), lambda b, i: (row(b, i), COL_V))],
        out_specs=[ospec, ospec,
                   pl.BlockSpec((None, FOX_HEADS, None, FOX_HEAD_DIM, ts), lambda b, i: (b, 0, i, 0, 0))],
        out_shape=[out, out,
                   jax.ShapeDtypeStruct((bsz, FOX_HEADS, nt, FOX_HEAD_DIM, ts), BF16)],
        scratch_shapes=[pltpu.VMEM((8, SMALL), F32)],
        compiler_params=_cparams(("parallel", "arbitrary")),
        name="fox_prep",
    )(small, bias_row, u, u, u)


def _flash_kernel(q_ref, k_ref, vt_ref, g_ref, o_ref, acc_ref):
    qi = pl.program_id(2)
    t = q_ref.shape[0]
    q = q_ref[...]

    def block(kj, m, l, masked):
        k_blk = k_ref[pl.ds(pl.multiple_of(kj * t, t), t), :]
        s = lax.dot_general(k_blk, q, (((1,), (1,)), ((), ())),
                            preferred_element_type=F32)
        if masked:
            key = lax.broadcasted_iota(jnp.int32, (t, t), 0)
            qry = lax.broadcasted_iota(jnp.int32, (t, t), 1)
            s = jnp.where(key <= qry, s, NEG_INF)
        m_new = jnp.maximum(m, jnp.max(s, axis=0, keepdims=True))
        alpha = jnp.exp2(m - m_new)
        p = jnp.exp2(s - m_new)
        l_new = alpha * l + jnp.sum(p, axis=0, keepdims=True)
        acc_ref[...] = alpha * acc_ref[...] + jnp.dot(vt_ref[kj], p.astype(BF16),
                                                      preferred_element_type=F32)
        return m_new, l_new

    acc_ref[...] = jnp.zeros_like(acc_ref)
    m0 = jnp.full((1, t), -jnp.inf, F32)
    l0 = jnp.zeros((1, t), F32)
    m, l = lax.fori_loop(0, qi, lambda kj, c: block(kj, c[0], c[1], False), (m0, l0))
    m, l = block(qi, m, l, True)
    o = (acc_ref[...] / l).T
    o_ref[...] = (o * _silu(g_ref[...].astype(F32))).astype(o_ref.dtype)


def _flash(qp, kp, vt, u, bsz, s_len, t):
    nq = s_len // t
    return pl.pallas_call(
        _flash_kernel,
        grid=(bsz, FOX_HEADS, nq),
        in_specs=[
            pl.BlockSpec((None, None, t, FOX_AUG), lambda b, h, qi: (b, h, qi, 0)),
            pl.BlockSpec((None, None, s_len, FOX_AUG), lambda b, h, qi: (b, h, 0, 0)),
            pl.BlockSpec((None, None, nq, FOX_HEAD_DIM, t), lambda b, h, qi: (b, h, 0, 0, 0)),
            pl.BlockSpec((t, FOX_HEAD_DIM), lambda b, h, qi: (b * nq + qi, COL_GA * FOX_HEADS + h)),
        ],
        out_specs=pl.BlockSpec((t, FOX_HEAD_DIM), lambda b, h, qi: (b * nq + qi, h)),
        out_shape=jax.ShapeDtypeStruct((bsz * s_len, WIDTH), BF16),
        scratch_shapes=[pltpu.VMEM((FOX_HEAD_DIM, t), F32)],
        compiler_params=_cparams(("parallel", "parallel", "arbitrary")),
        name="fox_flash",
    )(qp, kp, vt, u)


def _causal_conv(ext_ref, w_ref, b_ref, taps, halo, t):
    acc = None
    for k in range(taps):
        off = halo - (taps - 1) + k
        term = w_ref[k:k + 1, :] * ext_ref[off:off + t, :]
        acc = term if acc is None else acc + term
    return acc + b_ref[...]


def _ssd_kernel(xs_ref, bc_ref, z_ref, small_ref, cwx_ref, cbx_ref, cwbc_ref, cbbc_ref,
                dtb_ref, alog_ref, dskip_ref, nw_ref, o_ref,
                extx_ref, extbc_ref, state_ref):
    t = xs_ref.shape[0]
    hp = SSM_HEADS // SSM_GROUPS * SSM_HEAD_DIM

    @pl.when(pl.program_id(1) == 0)
    def _():
        extx_ref[0:HALO, :] = jnp.zeros((HALO, WIDTH), F32)
        extbc_ref[0:HALO, :] = jnp.zeros((HALO, 512), F32)
        state_ref[...] = jnp.zeros_like(state_ref)

    extx_ref[HALO:HALO + t, :] = xs_ref[...].astype(F32)
    extbc_ref[HALO:HALO + t, :] = bc_ref[...].astype(F32)
    xs = _silu(_causal_conv(extx_ref, cwx_ref, cbx_ref, SSM_CONV, HALO, t))
    bc = _silu(_causal_conv(extbc_ref, cwbc_ref, cbbc_ref, SSM_CONV, HALO, t))
    extx_ref[0:HALO, :] = extx_ref[t:t + HALO, :]
    extbc_ref[0:HALO, :] = extbc_ref[t:t + HALO, :]

    dt = _softplus(small_ref[...] + dtb_ref[...])
    da = dt * (-jnp.exp(alog_ref[...]))
    tri = _tril_ones(t)
    cs = _exact_dot(tri.astype(BF16), da)
    er = lax.broadcasted_iota(jnp.int32, (SMALL, WIDTH), 0)
    ec = lax.broadcasted_iota(jnp.int32, (SMALL, WIDTH), 1)
    expand = (er == LANE_DT + ec // SSM_HEAD_DIM).astype(BF16)
    dt_e = _exact_dot_r(dt, expand)
    cs_e = _exact_dot_r(cs, expand)
    cs_last_e = cs_e[t - 1:t, :]

    xdt = xs * dt_e
    xdt_b = xdt.astype(BF16)
    xdec_b = (xdt * jnp.exp(cs_last_e - cs_e)).astype(BF16)
    cs_t = cs.T
    lane = lax.broadcasted_iota(jnp.int32, (t, 2 * SSM_HEAD_DIM), 1)
    lo_half = lane < SSM_HEAD_DIM

    y_parts = []
    for g in range(SSM_GROUPS):
        b_g = bc[:, g * SSM_STATE:(g + 1) * SSM_STATE].astype(BF16)
        c_g = bc[:, (SSM_GROUPS + g) * SSM_STATE:(SSM_GROUPS + g + 1) * SSM_STATE].astype(BF16)
        cb = lax.dot_general(c_g, b_g, (((1,), (1,)), ((), ())), preferred_element_type=F32)
        for pair in range(SSM_HEADS // SSM_GROUPS // 2):
            h0 = g * (SSM_HEADS // SSM_GROUPS) + 2 * pair
            gs = []
            for h in (h0, h0 + 1):
                seg = cs[:, LANE_DT + h:LANE_DT + h + 1] - cs_t[LANE_DT + h:LANE_DT + h + 1, :]
                lmat = jnp.exp(jnp.where(tri, seg, -jnp.inf))
                gs.append((cb * lmat).astype(BF16))
            xp = xdt_b[:, h0 * SSM_HEAD_DIM:(h0 + 2) * SSM_HEAD_DIM]
            zero = jnp.zeros_like(xp)
            rhs = jnp.concatenate([jnp.where(lo_half, xp, zero), jnp.where(lo_half, zero, xp)], axis=0)
            y_parts.append(jnp.dot(jnp.concatenate(gs, axis=1), rhs, preferred_element_type=F32))
    y_diag = jnp.concatenate(y_parts, axis=1)

    y_off_parts = []
    for g in range(SSM_GROUPS):
        b_g = bc[:, g * SSM_STATE:(g + 1) * SSM_STATE].astype(BF16)
        c_g = bc[:, (SSM_GROUPS + g) * SSM_STATE:(SSM_GROUPS + g + 1) * SSM_STATE].astype(BF16)
        prev = state_ref[g]
        y_off_parts.append(jnp.dot(c_g, prev.astype(BF16), preferred_element_type=F32))
        upd = lax.dot_general(b_g, xdec_b[:, g * hp:(g + 1) * hp], (((0,), (0,)), ((), ())),
                              preferred_element_type=F32)
        state_ref[g] = prev * jnp.exp(cs_last_e[:, g * hp:(g + 1) * hp]) + upd
    y_off = jnp.concatenate(y_off_parts, axis=1) * jnp.exp(cs_e)

    y = y_diag + y_off + xs * dskip_ref[...]
    v = y * _silu(z_ref[...].astype(F32))
    v = v * lax.rsqrt(jnp.mean(v * v, axis=-1, keepdims=True) + EPS)
    o_ref[...] = (v * nw_ref[...]).astype(o_ref.dtype)


def _ssd(u, small, p, bsz, s_len, t):
    nt = s_len // t
    row = lambda b, i: b * nt + i
    full = lambda shape: pl.BlockSpec(shape, lambda b, i: (0, 0))
    return pl.pallas_call(
        _ssd_kernel,
        grid=(bsz, nt),
        in_specs=[pl.BlockSpec((t, WIDTH), lambda b, i: (row(b, i), COL_XS)),
                  pl.BlockSpec((t, 512), lambda b, i: (row(b, i), COL_BC_512)),
                  pl.BlockSpec((t, WIDTH), lambda b, i: (row(b, i), COL_Z)),
                  pl.BlockSpec((t, SMALL), lambda b, i: (row(b, i), 0)),
                  full((SSM_CONV, WIDTH)), full((1, WIDTH)),
                  full((SSM_CONV, 512)), full((1, 512)),
                  full((1, SMALL)), full((1, SMALL)), full((1, WIDTH)), full((1, WIDTH))],
        out_specs=pl.BlockSpec((t, WIDTH), lambda b, i: (row(b, i), 0)),
        out_shape=jax.ShapeDtypeStruct((bsz * s_len, WIDTH), BF16),
        scratch_shapes=[pltpu.VMEM((HALO + t, WIDTH), F32),
                        pltpu.VMEM((HALO + t, 512), F32),
                        pltpu.VMEM((SSM_GROUPS, SSM_STATE, 512), F32)],
        compiler_params=_cparams(("parallel", "arbitrary")),
        name="ssd",
    )(u, u, u, small, p["cwx"], p["cbx"], p["cwbc"], p["cbbc"],
      p["dtb"], p["alog"], p["dskip"], p["ssm_nw"])


def _shortconv_kernel(b_ref, c_ref, x_ref, g_ref, w_ref, cb_ref, o_ref, ext_ref):
    t = x_ref.shape[0]

    @pl.when(pl.program_id(1) == 0)
    def _():
        ext_ref[0:HALO, :] = jnp.zeros((HALO, WIDTH), F32)

    ext_ref[HALO:HALO + t, :] = c_ref[...].astype(F32) * x_ref[...].astype(F32)
    conv = _causal_conv(ext_ref, w_ref, cb_ref, SC_CONV, HALO, t)
    ext_ref[0:HALO, :] = ext_ref[t:t + HALO, :]
    o_ref[...] = (b_ref[...].astype(F32) * conv * _silu(g_ref[...].astype(F32))).astype(o_ref.dtype)


def _shortconv(u, p, bsz, s_len, t):
    nt = s_len // t
    row = lambda b, i: b * nt + i
    col = lambda c: pl.BlockSpec((t, WIDTH), lambda b, i: (row(b, i), c))
    full = lambda shape: pl.BlockSpec(shape, lambda b, i: (0, 0))
    return pl.pallas_call(
        _shortconv_kernel,
        grid=(bsz, nt),
        in_specs=[col(COL_SCB), col(COL_SCC), col(COL_SCX), col(COL_GC),
                  full((SC_CONV, WIDTH)), full((1, WIDTH))],
        out_specs=pl.BlockSpec((t, WIDTH), lambda b, i: (row(b, i), 0)),
        out_shape=jax.ShapeDtypeStruct((bsz * s_len, WIDTH), BF16),
        scratch_shapes=[pltpu.VMEM((HALO + t, WIDTH), F32)],
        compiler_params=_cparams(("parallel", "arbitrary")),
        name="shortconv",
    )(u, u, u, u, p["sc_w"], p["sc_b"])


def _conformer_kernel(a_ref, gg_ref, g_ref, w_ref, cb_ref, lnw_ref, lnb_ref, o_ref, ext_ref):
    t = a_ref.shape[0]

    @pl.when(pl.program_id(1) == 0)
    def _():
        ext_ref[0:CF_HALO, :] = jnp.zeros((CF_HALO, WIDTH), F32)

    ext_ref[CF_HALO:CF_HALO + t, :] = a_ref[...].astype(F32) * _sigmoid(gg_ref[...].astype(F32))
    cf = _causal_conv(ext_ref, w_ref, cb_ref, CF_CONV, CF_HALO, t)
    ext_ref[0:CF_HALO, :] = ext_ref[t:t + CF_HALO, :]
    mu = jnp.mean(cf, axis=-1, keepdims=True)
    d = cf - mu
    var = jnp.mean(d * d, axis=-1, keepdims=True)
    ln = d * lax.rsqrt(var + EPS) * lnw_ref[...] + lnb_ref[...]
    o_ref[...] = (_silu(ln) * _silu(g_ref[...].astype(F32))).astype(o_ref.dtype)


def _conformer(u, p, bsz, s_len, t):
    nt = s_len // t
    row = lambda b, i: b * nt + i
    col = lambda c: pl.BlockSpec((t, WIDTH), lambda b, i: (row(b, i), c))
    full = lambda shape: pl.BlockSpec(shape, lambda b, i: (0, 0))
    return pl.pallas_call(
        _conformer_kernel,
        grid=(bsz, nt),
        in_specs=[col(COL_GLUA), col(COL_GLUG), col(COL_GD),
                  full((CF_CONV, WIDTH)), full((1, WIDTH)), full((1, WIDTH)), full((1, WIDTH))],
        out_specs=pl.BlockSpec((t, WIDTH), lambda b, i: (row(b, i), 0)),
        out_shape=jax.ShapeDtypeStruct((bsz * s_len, WIDTH), BF16),
        scratch_shapes=[pltpu.VMEM((CF_HALO + t, WIDTH), F32)],
        compiler_params=_cparams(("parallel", "arbitrary")),
        name="conformer",
    )(u, u, u, p["cf_w"], p["cf_b"], p["cf_lnw"], p["cf_lnb"])


def _merge_kernel(h_ref, ya_ref, yb_ref, yc_ref, yd_ref, wg_ref, bg_ref, wb_ref, o_ref):
    h = h_ref[...]
    acc = None
    for i, y_ref in enumerate((ya_ref, yb_ref, yc_ref, yd_ref)):
        gate = _sigmoid(jnp.dot(h, wg_ref[i], preferred_element_type=F32) + bg_ref[i:i + 1, :])
        term = gate * jnp.dot(y_ref[...], wb_ref[i], preferred_element_type=F32)
        acc = term if acc is None else acc + term
    o_ref[...] = acc.astype(o_ref.dtype)


def _merge(h, ys, wg, bg, wb, tm, tn):
    m = h.shape[0]
    yspec = pl.BlockSpec((tm, WIDTH), lambda i, j: (i, 0))
    return pl.pallas_call(
        _merge_kernel,
        grid=(m // tm, D_MODEL // tn),
        in_specs=[pl.BlockSpec((tm, D_MODEL), lambda i, j: (i, 0)),
                  yspec, yspec, yspec, yspec,
                  pl.BlockSpec((N_BRANCH, D_MODEL, tn), lambda i, j: (0, 0, j)),
                  pl.BlockSpec((N_BRANCH, tn), lambda i, j: (0, j)),
                  pl.BlockSpec((N_BRANCH, WIDTH, tn), lambda i, j: (0, 0, j))],
        out_specs=pl.BlockSpec((tm, tn), lambda i, j: (i, j)),
        out_shape=jax.ShapeDtypeStruct((m, D_MODEL), BF16),
        compiler_params=_cparams(("parallel", "arbitrary")),
        name="merge",
    )(h, *ys, wg, bg, wb)


def _out_kernel(x_ref, mg_ref, w_ref, nw_ref, *out_refs, last):
    x_new = x_ref[...] + jnp.dot(mg_ref[...], w_ref[...], preferred_element_type=F32)
    normed = x_new * lax.rsqrt(jnp.mean(x_new * x_new, axis=-1, keepdims=True) + EPS) * nw_ref[...]
    if last:
        out_refs[0][...] = normed.astype(out_refs[0].dtype)
    else:
        out_refs[0][...] = x_new
        out_refs[1][...] = normed.astype(out_refs[1].dtype)


def _out_proj(x2d, merged, w_out, next_norm_w, tm, last):
    m, d = x2d.shape
    row = pl.BlockSpec((tm, d), lambda i: (i, 0))
    if last:
        out_shape = [jax.ShapeDtypeStruct((m, d), F32)]
    else:
        out_shape = [jax.ShapeDtypeStruct((m, d), F32), jax.ShapeDtypeStruct((m, d), BF16)]
    return pl.pallas_call(
        functools.partial(_out_kernel, last=last),
        grid=(m // tm,),
        in_specs=[row, row,
                  pl.BlockSpec((d, d), lambda i: (0, 0)),
                  pl.BlockSpec((1, d), lambda i: (0, 0))],
        out_specs=[row] * len(out_shape),
        out_shape=out_shape,
        compiler_params=_cparams(("parallel",)),
        name="out_proj_last" if last else "out_proj",
    )(x2d, merged, w_out, next_norm_w.reshape(1, d))


def _pad_lanes(v, start):
    n = v.shape[-1]
    return jnp.pad(v.astype(F32), ((0, 0), (start, SMALL - start - n)))[:, None, :]


def _prep_w_in(w_in):
    scale = FOX_HEAD_DIM ** -0.5 * LOG2E
    o = 0
    parts = {}
    for name, size in (("q", 1024), ("k", 1024), ("v", 1024), ("f", FOX_HEADS), ("ga", 1024),
                       ("z", 1024), ("xbc", 1536), ("dt", SSM_HEADS), ("scb", 1024), ("scc", 1024),
                       ("scx", 1024), ("gc", 1024), ("glu", 2048), ("gd", 1024)):
        parts[name] = w_in[:, :, o:o + size]
        o += size
    main = jnp.concatenate(
        [parts["q"] * scale, parts["k"], parts["v"], parts["ga"], parts["z"],
         parts["scb"], parts["scc"], parts["scx"], parts["gc"], parts["glu"], parts["gd"],
         parts["xbc"]], axis=-1).astype(BF16)
    pad = jnp.zeros(w_in.shape[:2] + (SMALL - FOX_HEADS - SSM_HEADS,), w_in.dtype)
    small = jnp.concatenate([parts["f"], parts["dt"], pad], axis=-1).astype(BF16)
    return main, small


def _tiles(m, s_len):
    pick = lambda want, n: min(want, n)
    return dict(tm_in=pick(1024, m), tn_in=1536, tm_norm=pick(512, m),
                t_attn=pick(512, s_len), t_ssd=pick(128, s_len), t_conv=pick(256, s_len),
                tm_merge=pick(512, m), tn_merge=512, tm_out=pick(512, m))


def kernel(x, norm_w, w_in, fg_bias, ssm_conv_w, ssm_conv_b, dt_bias, a_log, d_skip, ssm_norm_w,
           sc_conv_w, sc_conv_b, cf_conv_w, cf_conv_b, cf_ln_w, cf_ln_b, w_gate, b_gate, w_branch,
           w_out, final_norm_w):
    bsz, s_len, d = x.shape
    depth = w_in.shape[0]
    m = bsz * s_len
    tl = _tiles(m, s_len)

    w_main, w_small = _prep_w_in(w_in)
    wg = w_gate.astype(BF16)
    wb = w_branch.astype(BF16)
    wo = w_out.astype(BF16)
    fg_row = _pad_lanes(fg_bias, LANE_F)
    dtb_row = _pad_lanes(dt_bias, LANE_DT)
    alog_row = _pad_lanes(a_log, LANE_DT)
    dskip_row = jnp.repeat(d_skip.astype(F32), SSM_HEAD_DIM, axis=-1)[:, None, :]
    row = lambda a: a.astype(F32)[:, None, :]

    x2d = x.reshape(m, d)
    h = _rmsnorm(x2d, norm_w[0], tl["tm_norm"])
    for l in range(depth):
        u = _matmul(h, w_main[l], tl["tm_in"], tl["tn_in"], BF16, "in_proj")
        small = _matmul(h, w_small[l], tl["tm_in"], SMALL, F32, "in_proj_small")
        qp, kp, vt = _fox_prep(u, small, fg_row[l], bsz, s_len, tl["t_attn"])
        y_a = _flash(qp, kp, vt, u, bsz, s_len, tl["t_attn"])
        ssd_p = dict(cwx=ssm_conv_w[l, :, :WIDTH], cbx=ssm_conv_b[l, None, :WIDTH],
                     cwbc=ssm_conv_w[l, :, WIDTH:], cbbc=ssm_conv_b[l, None, WIDTH:],
                     dtb=dtb_row[l], alog=alog_row[l], dskip=dskip_row[l], ssm_nw=row(ssm_norm_w)[l])
        y_b = _ssd(u, small, ssd_p, bsz, s_len, tl["t_ssd"])
        y_c = _shortconv(u, dict(sc_w=sc_conv_w[l], sc_b=sc_conv_b[l, None, :]), bsz, s_len, tl["t_conv"])
        y_d = _conformer(u, dict(cf_w=cf_conv_w[l], cf_b=cf_conv_b[l, None, :],
                                 cf_lnw=cf_ln_w[l, None, :], cf_lnb=cf_ln_b[l, None, :]),
                         bsz, s_len, tl["t_conv"])
        merged = _merge(h, (y_a, y_b, y_c, y_d), wg[l], b_gate[l], wb[l], tl["tm_merge"], tl["tn_merge"])
        last = l == depth - 1
        nw = final_norm_w if last else norm_w[l + 1]
        outs = _out_proj(x2d, merged, wo[l], nw, tl["tm_out"], last)
        if last:
            return outs[0].reshape(bsz, s_len, d)
        x2d, h = outs
```

```python
import functools

import jax
import jax.numpy as jnp
from jax import lax
from jax.experimental import pallas as pl
from jax.experimental.pallas import tpu as pltpu

F32 = jnp.float32
BF16 = jnp.bfloat16

D_MODEL = 2048
EPS = 1e-6
NEG_INF = -1e30
LOG2E = 1.4426950408889634

FOX_HEADS = 8
FOX_HEAD_DIM = 128
FOX_AUG = 2 * FOX_HEAD_DIM
SSM_HEADS = 16
SSM_HEAD_DIM = 64
SSM_GROUPS = 2
SSM_STATE = 128
SSM_CONV = 4
SC_CONV = 3
CF_CONV = 31
WIDTH = 1024
N_BRANCH = 4

COL_Q, COL_K, COL_V, COL_GA, COL_Z = 0, 1, 2, 3, 4
COL_SCB, COL_SCC, COL_SCX, COL_GC = 5, 6, 7, 8
COL_GLUA, COL_GLUG, COL_GD, COL_XS = 9, 10, 11, 12
COL_BC_512 = 26
N_MAIN = 13 * WIDTH + 512
LANE_F = 0
LANE_DT = 8
SMALL = 128

SUBLANES = 8
HALO = SUBLANES
CF_HALO = 32

V7X_VMEM_LIMIT = 56 * 1024 * 1024


def _cparams(sem):
    return pltpu.CompilerParams(dimension_semantics=sem, vmem_limit_bytes=V7X_VMEM_LIMIT)


def _sigmoid(x):
    return 1.0 / (1.0 + jnp.exp(-x))


def _silu(x):
    return x * _sigmoid(x)


def _softplus(x):
    return jnp.maximum(x, 0.0) + jnp.log1p(jnp.exp(-jnp.abs(x)))


def _split3(x):
    hi = x.astype(BF16)
    r1 = x - hi.astype(F32)
    mid = r1.astype(BF16)
    lo = (r1 - mid.astype(F32)).astype(BF16)
    return hi, mid, lo


def _exact_dot(m_bf16, x):
    hi, mid, lo = _split3(x)
    d = lambda p: jnp.dot(m_bf16, p, preferred_element_type=F32)
    return d(hi) + d(mid) + d(lo)


def _exact_dot_r(x, m_bf16):
    hi, mid, lo = _split3(x)
    d = lambda p: jnp.dot(p, m_bf16, preferred_element_type=F32)
    return d(hi) + d(mid) + d(lo)


def _tril_ones(n):
    r = lax.broadcasted_iota(jnp.int32, (n, n), 0)
    c = lax.broadcasted_iota(jnp.int32, (n, n), 1)
    return r >= c


def _rmsnorm_kernel(x_ref, w_ref, h_ref):
    x = x_ref[...]
    y = x * lax.rsqrt(jnp.mean(x * x, axis=-1, keepdims=True) + EPS)
    h_ref[...] = (y * w_ref[...]).astype(h_ref.dtype)


def _rmsnorm(x2d, w, tm):
    m, d = x2d.shape
    return pl.pallas_call(
        _rmsnorm_kernel,
        grid=(m // tm,),
        in_specs=[pl.BlockSpec((tm, d), lambda i: (i, 0)),
                  pl.BlockSpec((1, d), lambda i: (0, 0))],
        out_specs=pl.BlockSpec((tm, d), lambda i: (i, 0)),
        out_shape=jax.ShapeDtypeStruct((m, d), BF16),
        compiler_params=_cparams(("parallel",)),
        name="rmsnorm",
    )(x2d, w.reshape(1, d))


def _matmul_kernel(h_ref, w_ref, o_ref):
    o_ref[...] = jnp.dot(h_ref[...], w_ref[...], preferred_element_type=F32).astype(o_ref.dtype)


def _matmul(h, w, tm, tn, out_dtype, name):
    m, k = h.shape
    n = w.shape[1]
    return pl.pallas_call(
        _matmul_kernel,
        grid=(m // tm, n // tn),
        in_specs=[pl.BlockSpec((tm, k), lambda i, j: (i, 0)),
                  pl.BlockSpec((k, tn), lambda i, j: (0, j))],
        out_specs=pl.BlockSpec((tm, tn), lambda i, j: (i, j)),
        out_shape=jax.ShapeDtypeStruct((m, n), out_dtype),
        compiler_params=_cparams(("parallel", "arbitrary")),
        name=name,
    )(h, w)


def _fox_prep_kernel(small_ref, bias_ref, q_ref, k_ref, v_ref, qt_ref, kp_ref, vt_ref, carry_ref):
    ts = small_ref.shape[0]

    @pl.when(pl.program_id(1) == 0)
    def _():
        carry_ref[...] = jnp.zeros_like(carry_ref)

    f = small_ref[...] + bias_ref[...]
    logf = -_softplus(-f)
    tri = _tril_ones(ts).astype(BF16)
    c = _exact_dot(tri, logf) + carry_ref[0:1, :]
    carry_ref[...] = jnp.broadcast_to(c[ts - 1:ts, :], carry_ref.shape)
    c = c * LOG2E

    def split(x):
        hi = x.astype(BF16).astype(F32)
        r1 = x - hi
        mid = r1.astype(BF16).astype(F32)
        return hi, mid, (r1 - mid).astype(BF16).astype(F32)

    c_t = c.T
    lane = lax.broadcasted_iota(jnp.int32, (ts, FOX_HEAD_DIM), 1)
    sub = lax.broadcasted_iota(jnp.int32, (FOX_HEAD_DIM, ts), 0)
    ones_q = jnp.where(sub < 6, 1.0, 0.0)
    ones_k = jnp.where(lane < 3, 1.0, 0.0)
    for h in range(FOX_HEADS):
        hi, mid, lo = split(c_t[LANE_F + h:LANE_F + h + 1, :])
        aq_t = jnp.where(sub == 0, hi, jnp.where(sub == 1, mid, jnp.where(sub == 2, lo, ones_q)))
        hi, mid, lo = split(c[:, LANE_F + h:LANE_F + h + 1])
        ak = jnp.where(lane == 3, -hi, jnp.where(lane == 4, -mid, jnp.where(lane == 5, -lo, ones_k)))
        sl = slice(h * FOX_HEAD_DIM, (h + 1) * FOX_HEAD_DIM)
        qt_ref[h, 0:FOX_HEAD_DIM, :] = q_ref[:, sl].astype(F32).T.astype(BF16)
        qt_ref[h, FOX_HEAD_DIM:FOX_AUG, :] = aq_t.astype(BF16)
        kp_ref[h, :, 0:FOX_HEAD_DIM] = k_ref[:, sl]
        kp_ref[h, :, FOX_HEAD_DIM:FOX_AUG] = ak.astype(BF16)
        vt_ref[h] = v_ref[:, sl].astype(F32).T.astype(BF16)


def _fox_prep(u, small, bias_row, bsz, s_len, ts):
    nt = s_len // ts
    row = lambda b, i: b * nt + i
    out = jax.ShapeDtypeStruct((bsz, FOX_HEADS, s_len, FOX_AUG), BF16)
    ospec = pl.BlockSpec((None, FOX_HEADS, ts, FOX_AUG), lambda b, i: (b, 0, i, 0))
    out_t = jax.ShapeDtypeStruct((bsz, FOX_HEADS, FOX_AUG, s_len), BF16)
    ospec_t = pl.BlockSpec((None, FOX_HEADS, FOX_AUG, ts), lambda b, i: (b, 0, 0, i))
    return pl.pallas_call(
        _fox_prep_kernel,
        grid=(bsz, nt),
        in_specs=[pl.BlockSpec((ts, SMALL), lambda b, i: (row(b, i), 0)),
                  pl.BlockSpec((1, SMALL), lambda b, i: (0, 0)),
                  pl.BlockSpec((ts, WIDTH), lambda b, i: (row(b, i), COL_Q)),
                  pl.BlockSpec((ts, WIDTH), lambda b, i: (row(b, i), COL_K)),
                  pl.BlockSpec((ts, WIDTH), lambda b, i: (row(b, i), COL_V))],
        out_specs=[ospec_t, ospec,
                   pl.BlockSpec((None, FOX_HEADS, None, FOX_HEAD_DIM, ts), lambda b, i: (b, 0, i, 0, 0))],
        out_shape=[out_t, out,
                   jax.ShapeDtypeStruct((bsz, FOX_HEADS, nt, FOX_HEAD_DIM, ts), BF16)],
        scratch_shapes=[pltpu.VMEM((8, SMALL), F32)],
        compiler_params=_cparams(("parallel", "arbitrary")),
        name="fox_prep",
    )(small, bias_row, u, u, u)


def _flash_kernel(qt_ref, k_ref, vt_ref, g_ref, o_ref, acc_ref, s0_ref, s1_ref):
    n = pl.program_id(2)
    t = qt_ref.shape[1]
    qt = qt_ref[...]

    def scores(kj, s_ref):
        k_blk = k_ref[pl.ds(pl.multiple_of(kj * t, t), t), :]
        s_ref[...] = jnp.dot(k_blk, qt, preferred_element_type=F32)

    def softmax_pv(kj, s_ref, m, l, masked):
        s = s_ref[...]
        if masked:
            key = lax.broadcasted_iota(jnp.int32, (t, t), 0)
            qry = lax.broadcasted_iota(jnp.int32, (t, t), 1)
            s = jnp.where(key <= qry, s, NEG_INF)
        m_new = jnp.maximum(m, jnp.max(s, axis=0, keepdims=True))
        alpha = jnp.exp2(m - m_new)
        p = jnp.exp2(s - m_new)
        l_new = alpha * l + jnp.sum(p, axis=0, keepdims=True)
        acc_ref[...] = alpha * acc_ref[...] + jnp.dot(vt_ref[kj], p.astype(BF16),
                                                      preferred_element_type=F32)
        return m_new, l_new

    acc_ref[...] = jnp.zeros_like(acc_ref)
    m0 = jnp.full((1, t), -jnp.inf, F32)
    l0 = jnp.zeros((1, t), F32)
    scores(0, s0_ref)

    def pair(p, c):
        m, l = c
        scores(2 * p + 1, s1_ref)
        m, l = softmax_pv(2 * p, s0_ref, m, l, False)
        scores(2 * p + 2, s0_ref)
        return softmax_pv(2 * p + 1, s1_ref, m, l, False)

    m, l = lax.fori_loop(0, n // 2, pair, (m0, l0))

    def tail_odd(m, l):
        scores(n, s1_ref)
        m, l = softmax_pv(n - 1, s0_ref, m, l, False)
        return softmax_pv(n, s1_ref, m, l, True)

    def tail_even(m, l):
        return softmax_pv(n, s0_ref, m, l, True)

    m, l = lax.cond(n % 2 == 1, tail_odd, tail_even, m, l)
    o = (acc_ref[...] / l).T
    o_ref[...] = (o * _silu(g_ref[...].astype(F32))).astype(o_ref.dtype)


def _flash(qt, kp, vt, u, bsz, s_len, t):
    nq = s_len // t
    return pl.pallas_call(
        _flash_kernel,
        grid=(bsz, FOX_HEADS, nq),
        in_specs=[
            pl.BlockSpec((None, None, FOX_AUG, t), lambda b, h, qi: (b, h, 0, qi)),
            pl.BlockSpec((None, None, s_len, FOX_AUG), lambda b, h, qi: (b, h, 0, 0)),
            pl.BlockSpec((None, None, nq, FOX_HEAD_DIM, t), lambda b, h, qi: (b, h, 0, 0, 0)),
            pl.BlockSpec((t, FOX_HEAD_DIM), lambda b, h, qi: (b * nq + qi, COL_GA * FOX_HEADS + h)),
        ],
        out_specs=pl.BlockSpec((t, FOX_HEAD_DIM), lambda b, h, qi: (b * nq + qi, h)),
        out_shape=jax.ShapeDtypeStruct((bsz * s_len, WIDTH), BF16),
        scratch_shapes=[pltpu.VMEM((FOX_HEAD_DIM, t), F32), pltpu.VMEM((t, t), F32),
                        pltpu.VMEM((t, t), F32)],
        compiler_params=_cparams(("parallel", "parallel", "arbitrary")),
        name="fox_flash",
    )(qt, kp, vt, u)


def _causal_conv(ext_ref, w_ref, b_ref, taps, halo, t):
    acc = None
    for k in range(taps):
        off = halo - (taps - 1) + k
        term = w_ref[k:k + 1, :] * ext_ref[off:off + t, :]
        acc = term if acc is None else acc + term
    return acc + b_ref[...]


def _causal_conv_long(ext_ref, shift_ref, w_ref, b_ref, taps, halo, t):
    acc = None
    for r in range(SUBLANES):
        ks = [k for k in range(taps) if (taps - 1 - k) % SUBLANES == r]
        if not ks:
            continue
        max_a = max((taps - 1 - k) // SUBLANES for k in ks)
        rows = t + SUBLANES * max_a
        start = halo - SUBLANES * max_a - r
        shift_ref[0:rows, :] = ext_ref[start:start + rows, :]
        for k in ks:
            off = SUBLANES * (max_a - (taps - 1 - k) // SUBLANES)
            term = w_ref[k:k + 1, :] * shift_ref[off:off + t, :]
            acc = term if acc is None else acc + term
    return acc + b_ref[...]


def _ssd_kernel(xs_ref, bc_ref, z_ref, small_ref, cwx_ref, cbx_ref, cwbc_ref, cbbc_ref,
                dtb_ref, alog_ref, dskip_ref, nw_ref, o_ref,
                extx_ref, extbc_ref, state_ref):
    t = xs_ref.shape[0]
    hp = SSM_HEADS // SSM_GROUPS * SSM_HEAD_DIM

    @pl.when(pl.program_id(1) == 0)
    def _():
        extx_ref[0:HALO, :] = jnp.zeros((HALO, WIDTH), F32)
        extbc_ref[0:HALO, :] = jnp.zeros((HALO, 512), F32)
        state_ref[...] = jnp.zeros_like(state_ref)

    extx_ref[HALO:HALO + t, :] = xs_ref[...].astype(F32)
    extbc_ref[HALO:HALO + t, :] = bc_ref[...].astype(F32)
    xs = _silu(_causal_conv(extx_ref, cwx_ref, cbx_ref, SSM_CONV, HALO, t))
    bc = _silu(_causal_conv(extbc_ref, cwbc_ref, cbbc_ref, SSM_CONV, HALO, t))
    extx_ref[0:HALO, :] = extx_ref[t:t + HALO, :]
    extbc_ref[0:HALO, :] = extbc_ref[t:t + HALO, :]

    dt = _softplus(small_ref[...] + dtb_ref[...])
    da = dt * (-jnp.exp(alog_ref[...]))
    tri = _tril_ones(t)
    cs = _exact_dot(tri.astype(BF16), da)
    er = lax.broadcasted_iota(jnp.int32, (SMALL, WIDTH), 0)
    ec = lax.broadcasted_iota(jnp.int32, (SMALL, WIDTH), 1)
    expand = (er == LANE_DT + ec // SSM_HEAD_DIM).astype(BF16)
    dt_e = _exact_dot_r(dt, expand)
    cs_e = _exact_dot_r(cs, expand)
    cs_last_e = cs_e[t - 1:t, :]

    xdt = xs * dt_e
    xdt_b = xdt.astype(BF16)
    xdec_b = (xdt * jnp.exp(cs_last_e - cs_e)).astype(BF16)
    cs_t = cs.T
    lane = lax.broadcasted_iota(jnp.int32, (t, 2 * SSM_HEAD_DIM), 1)
    lo_half = lane < SSM_HEAD_DIM

    y_parts = []
    for g in range(SSM_GROUPS):
        b_g = bc[:, g * SSM_STATE:(g + 1) * SSM_STATE].astype(BF16)
        c_g = bc[:, (SSM_GROUPS + g) * SSM_STATE:(SSM_GROUPS + g + 1) * SSM_STATE].astype(BF16)
        cb = lax.dot_general(c_g, b_g, (((1,), (1,)), ((), ())), preferred_element_type=F32)
        for pair in range(SSM_HEADS // SSM_GROUPS // 2):
            h0 = g * (SSM_HEADS // SSM_GROUPS) + 2 * pair
            gs = []
            for h in (h0, h0 + 1):
                seg = cs[:, LANE_DT + h:LANE_DT + h + 1] - cs_t[LANE_DT + h:LANE_DT + h + 1, :]
                lmat = jnp.exp(jnp.where(tri, seg, -jnp.inf))
                gs.append((cb * lmat).astype(BF16))
            xp = xdt_b[:, h0 * SSM_HEAD_DIM:(h0 + 2) * SSM_HEAD_DIM]
            zero = jnp.zeros_like(xp)
            rhs = jnp.concatenate([jnp.where(lo_half, xp, zero), jnp.where(lo_half, zero, xp)], axis=0)
            y_parts.append(jnp.dot(jnp.concatenate(gs, axis=1), rhs, preferred_element_type=F32))
    y_diag = jnp.concatenate(y_parts, axis=1)

    y_off_parts = []
    for g in range(SSM_GROUPS):
        b_g = bc[:, g * SSM_STATE:(g + 1) * SSM_STATE].astype(BF16)
        c_g = bc[:, (SSM_GROUPS + g) * SSM_STATE:(SSM_GROUPS + g + 1) * SSM_STATE].astype(BF16)
        prev = state_ref[g]
        y_off_parts.append(jnp.dot(c_g, prev.astype(BF16), preferred_element_type=F32))
        upd = lax.dot_general(b_g, xdec_b[:, g * hp:(g + 1) * hp], (((0,), (0,)), ((), ())),
                              preferred_element_type=F32)
        state_ref[g] = prev * jnp.exp(cs_last_e[:, g * hp:(g + 1) * hp]) + upd
    y_off = jnp.concatenate(y_off_parts, axis=1) * jnp.exp(cs_e)

    y = y_diag + y_off + xs * dskip_ref[...]
    v = y * _silu(z_ref[...].astype(F32))
    v = v * lax.rsqrt(jnp.mean(v * v, axis=-1, keepdims=True) + EPS)
    o_ref[...] = (v * nw_ref[...]).astype(o_ref.dtype)


def _ssd(u, small, p, bsz, s_len, t):
    nt = s_len // t
    row = lambda b, i: b * nt + i
    full = lambda shape: pl.BlockSpec(shape, lambda b, i: (0, 0))
    return pl.pallas_call(
        _ssd_kernel,
        grid=(bsz, nt),
        in_specs=[pl.BlockSpec((t, WIDTH), lambda b, i: (row(b, i), COL_XS)),
                  pl.BlockSpec((t, 512), lambda b, i: (row(b, i), COL_BC_512)),
                  pl.BlockSpec((t, WIDTH), lambda b, i: (row(b, i), COL_Z)),
                  pl.BlockSpec((t, SMALL), lambda b, i: (row(b, i), 0)),
                  full((SSM_CONV, WIDTH)), full((1, WIDTH)),
                  full((SSM_CONV, 512)), full((1, 512)),
                  full((1, SMALL)), full((1, SMALL)), full((1, WIDTH)), full((1, WIDTH))],
        out_specs=pl.BlockSpec((t, WIDTH), lambda b, i: (row(b, i), 0)),
        out_shape=jax.ShapeDtypeStruct((bsz * s_len, WIDTH), BF16),
        scratch_shapes=[pltpu.VMEM((HALO + t, WIDTH), F32),
                        pltpu.VMEM((HALO + t, 512), F32),
                        pltpu.VMEM((SSM_GROUPS, SSM_STATE, 512), F32)],
        compiler_params=_cparams(("parallel", "arbitrary")),
        name="ssd",
    )(u, u, u, small, p["cwx"], p["cbx"], p["cwbc"], p["cbbc"],
      p["dtb"], p["alog"], p["dskip"], p["ssm_nw"])


def _shortconv_kernel(b_ref, c_ref, x_ref, g_ref, w_ref, cb_ref, o_ref, ext_ref):
    t = x_ref.shape[0]

    @pl.when(pl.program_id(1) == 0)
    def _():
        ext_ref[0:HALO, :] = jnp.zeros((HALO, WIDTH), F32)

    ext_ref[HALO:HALO + t, :] = c_ref[...].astype(F32) * x_ref[...].astype(F32)
    conv = _causal_conv(ext_ref, w_ref, cb_ref, SC_CONV, HALO, t)
    ext_ref[0:HALO, :] = ext_ref[t:t + HALO, :]
    o_ref[...] = (b_ref[...].astype(F32) * conv * _silu(g_ref[...].astype(F32))).astype(o_ref.dtype)


def _shortconv(u, p, bsz, s_len, t):
    nt = s_len // t
    row = lambda b, i: b * nt + i
    col = lambda c: pl.BlockSpec((t, WIDTH), lambda b, i: (row(b, i), c))
    full = lambda shape: pl.BlockSpec(shape, lambda b, i: (0, 0))
    return pl.pallas_call(
        _shortconv_kernel,
        grid=(bsz, nt),
        in_specs=[col(COL_SCB), col(COL_SCC), col(COL_SCX), col(COL_GC),
                  full((SC_CONV, WIDTH)), full((1, WIDTH))],
        out_specs=pl.BlockSpec((t, WIDTH), lambda b, i: (row(b, i), 0)),
        out_shape=jax.ShapeDtypeStruct((bsz * s_len, WIDTH), BF16),
        scratch_shapes=[pltpu.VMEM((HALO + t, WIDTH), F32)],
        compiler_params=_cparams(("parallel", "arbitrary")),
        name="shortconv",
    )(u, u, u, u, p["sc_w"], p["sc_b"])


def _conformer_kernel(a_ref, gg_ref, g_ref, w_ref, cb_ref, lnw_ref, lnb_ref, o_ref, ext_ref, shift_ref):
    t = a_ref.shape[0]

    @pl.when(pl.program_id(1) == 0)
    def _():
        ext_ref[0:CF_HALO, :] = jnp.zeros((CF_HALO, WIDTH), F32)

    ext_ref[CF_HALO:CF_HALO + t, :] = a_ref[...].astype(F32) * _sigmoid(gg_ref[...].astype(F32))
    cf = _causal_conv_long(ext_ref, shift_ref, w_ref, cb_ref, CF_CONV, CF_HALO, t)
    ext_ref[0:CF_HALO, :] = ext_ref[t:t + CF_HALO, :]
    mu = jnp.mean(cf, axis=-1, keepdims=True)
    d = cf - mu
    var = jnp.mean(d * d, axis=-1, keepdims=True)
    ln = d * lax.rsqrt(var + EPS) * lnw_ref[...] + lnb_ref[...]
    o_ref[...] = (_silu(ln) * _silu(g_ref[...].astype(F32))).astype(o_ref.dtype)


def _conformer(u, p, bsz, s_len, t):
    nt = s_len // t
    row = lambda b, i: b * nt + i
    col = lambda c: pl.BlockSpec((t, WIDTH), lambda b, i: (row(b, i), c))
    full = lambda shape: pl.BlockSpec(shape, lambda b, i: (0, 0))
    return pl.pallas_call(
        _conformer_kernel,
        grid=(bsz, nt),
        in_specs=[col(COL_GLUA), col(COL_GLUG), col(COL_GD),
                  full((CF_CONV, WIDTH)), full((1, WIDTH)), full((1, WIDTH)), full((1, WIDTH))],
        out_specs=pl.BlockSpec((t, WIDTH), lambda b, i: (row(b, i), 0)),
        out_shape=jax.ShapeDtypeStruct((bsz * s_len, WIDTH), BF16),
        scratch_shapes=[pltpu.VMEM((CF_HALO + t, WIDTH), F32), pltpu.VMEM((CF_HALO + t, WIDTH), F32)],
        compiler_params=_cparams(("parallel", "arbitrary")),
        name="conformer",
    )(u, u, u, p["cf_w"], p["cf_b"], p["cf_lnw"], p["cf_lnb"])


def _merge_kernel(h_ref, ya_ref, yb_ref, yc_ref, yd_ref, wg_ref, bg_ref, wb_ref, o_ref):
    h = h_ref[...]
    acc = None
    for i, y_ref in enumerate((ya_ref, yb_ref, yc_ref, yd_ref)):
        gate = _sigmoid(jnp.dot(h, wg_ref[i], preferred_element_type=F32) + bg_ref[i:i + 1, :])
        term = gate * jnp.dot(y_ref[...], wb_ref[i], preferred_element_type=F32)
        acc = term if acc is None else acc + term
    o_ref[...] = acc.astype(o_ref.dtype)


def _merge(h, ys, wg, bg, wb, tm, tn):
    m = h.shape[0]
    yspec = pl.BlockSpec((tm, WIDTH), lambda i, j: (i, 0))
    return pl.pallas_call(
        _merge_kernel,
        grid=(m // tm, D_MODEL // tn),
        in_specs=[pl.BlockSpec((tm, D_MODEL), lambda i, j: (i, 0)),
                  yspec, yspec, yspec, yspec,
                  pl.BlockSpec((N_BRANCH, D_MODEL, tn), lambda i, j: (0, 0, j)),
                  pl.BlockSpec((N_BRANCH, tn), lambda i, j: (0, j)),
                  pl.BlockSpec((N_BRANCH, WIDTH, tn), lambda i, j: (0, 0, j))],
        out_specs=pl.BlockSpec((tm, tn), lambda i, j: (i, j)),
        out_shape=jax.ShapeDtypeStruct((m, D_MODEL), BF16),
        compiler_params=_cparams(("parallel", "arbitrary")),
        name="merge",
    )(h, *ys, wg, bg, wb)


def _out_kernel(x_ref, mg_ref, w_ref, nw_ref, *out_refs, last):
    x_new = x_ref[...] + jnp.dot(mg_ref[...], w_ref[...], preferred_element_type=F32)
    normed = x_new * lax.rsqrt(jnp.mean(x_new * x_new, axis=-1, keepdims=True) + EPS) * nw_ref[...]
    if last:
        out_refs[0][...] = normed.astype(out_refs[0].dtype)
    else:
        out_refs[0][...] = x_new
        out_refs[1][...] = normed.astype(out_refs[1].dtype)


def _out_proj(x2d, merged, w_out, next_norm_w, tm, last):
    m, d = x2d.shape
    row = pl.BlockSpec((tm, d), lambda i: (i, 0))
    if last:
        out_shape = [jax.ShapeDtypeStruct((m, d), F32)]
    else:
        out_shape = [jax.ShapeDtypeStruct((m, d), F32), jax.ShapeDtypeStruct((m, d), BF16)]
    return pl.pallas_call(
        functools.partial(_out_kernel, last=last),
        grid=(m // tm,),
        in_specs=[row, row,
                  pl.BlockSpec((d, d), lambda i: (0, 0)),
                  pl.BlockSpec((1, d), lambda i: (0, 0))],
        out_specs=[row] * len(out_shape),
        out_shape=out_shape,
        compiler_params=_cparams(("parallel",)),
        name="out_proj_last" if last else "out_proj",
    )(x2d, merged, w_out, next_norm_w.reshape(1, d))


def _pad_lanes(v, start):
    n = v.shape[-1]
    return jnp.pad(v.astype(F32), ((0, 0), (start, SMALL - start - n)))[:, None, :]


def _w_relayout_kernel(w_ref, main_ref, small_ref):
    tr = w_ref.shape[0]
    scale = FOX_HEAD_DIM ** -0.5 * LOG2E
    f0 = 3 * WIDTH
    dt0 = f0 + FOX_HEADS + 2 * WIDTH + WIDTH + 512
    xbc0 = f0 + FOX_HEADS + 2 * WIDTH
    rest0 = dt0 + SSM_HEADS
    main_ref[:, 0:WIDTH] = (w_ref[:, 0:WIDTH] * scale).astype(BF16)
    main_ref[:, WIDTH:f0] = w_ref[:, WIDTH:f0].astype(BF16)
    main_ref[:, f0:f0 + 2 * WIDTH] = w_ref[:, f0 + FOX_HEADS:xbc0].astype(BF16)
    main_ref[:, 5 * WIDTH:12 * WIDTH] = w_ref[:, rest0:rest0 + 7 * WIDTH].astype(BF16)
    main_ref[:, 12 * WIDTH:N_MAIN] = w_ref[:, xbc0:dt0].astype(BF16)
    pad = jnp.zeros((tr, SMALL - FOX_HEADS - SSM_HEADS), F32)
    small_ref[...] = jnp.concatenate([w_ref[:, f0:f0 + FOX_HEADS], w_ref[:, dt0:rest0], pad],
                                     axis=1).astype(BF16)


def _w_relayout(w_in, tr):
    depth, d, n_in = w_in.shape
    return pl.pallas_call(
        _w_relayout_kernel,
        grid=(depth, d // tr),
        in_specs=[pl.BlockSpec((None, tr, n_in), lambda l, i: (l, i, 0))],
        out_specs=[pl.BlockSpec((None, tr, N_MAIN), lambda l, i: (l, i, 0)),
                   pl.BlockSpec((None, tr, SMALL), lambda l, i: (l, i, 0))],
        out_shape=[jax.ShapeDtypeStruct((depth, d, N_MAIN), BF16),
                   jax.ShapeDtypeStruct((depth, d, SMALL), BF16)],
        compiler_params=_cparams(("parallel", "parallel")),
        name="w_relayout",
    )(w_in)


def _tiles(m, s_len):
    pick = lambda want, n: min(want, n)
    return dict(t_relayout=128, tm_in=pick(1024, m), tn_in=1536, tm_norm=pick(512, m),
                t_attn=pick(512, s_len), t_ssd=pick(128, s_len), t_conv=pick(256, s_len),
                tm_merge=pick(512, m), tn_merge=512, tm_out=pick(512, m))


def kernel(x, norm_w, w_in, fg_bias, ssm_conv_w, ssm_conv_b, dt_bias, a_log, d_skip, ssm_norm_w,
           sc_conv_w, sc_conv_b, cf_conv_w, cf_conv_b, cf_ln_w, cf_ln_b, w_gate, b_gate, w_branch,
           w_out, final_norm_w):
    bsz, s_len, d = x.shape
    depth = w_in.shape[0]
    m = bsz * s_len
    tl = _tiles(m, s_len)

    w_main, w_small = _w_relayout(w_in, tl["t_relayout"])
    wg = w_gate.astype(BF16)
    wb = w_branch.astype(BF16)
    wo = w_out.astype(BF16)
    fg_row = _pad_lanes(fg_bias, LANE_F)
    dtb_row = _pad_lanes(dt_bias, LANE_DT)
    alog_row = _pad_lanes(a_log, LANE_DT)
    dskip_row = jnp.repeat(d_skip.astype(F32), SSM_HEAD_DIM, axis=-1)[:, None, :]
    row = lambda a: a.astype(F32)[:, None, :]

    x2d = x.reshape(m, d)
    h = _rmsnorm(x2d, norm_w[0], tl["tm_norm"])
    for l in range(depth):
        u = _matmul(h, w_main[l], tl["tm_in"], tl["tn_in"], BF16, "in_proj")
        small = _matmul(h, w_small[l], tl["tm_in"], SMALL, F32, "in_proj_small")
        qp, kp, vt = _fox_prep(u, small, fg_row[l], bsz, s_len, tl["t_attn"])
        y_a = _flash(qp, kp, vt, u, bsz, s_len, tl["t_attn"])
        ssd_p = dict(cwx=ssm_conv_w[l, :, :WIDTH], cbx=ssm_conv_b[l, None, :WIDTH],
                     cwbc=ssm_conv_w[l, :, WIDTH:], cbbc=ssm_conv_b[l, None, WIDTH:],
                     dtb=dtb_row[l], alog=alog_row[l], dskip=dskip_row[l], ssm_nw=row(ssm_norm_w)[l])
        y_b = _ssd(u, small, ssd_p, bsz, s_len, tl["t_ssd"])
        y_c = _shortconv(u, dict(sc_w=sc_conv_w[l], sc_b=sc_conv_b[l, None, :]), bsz, s_len, tl["t_conv"])
        y_d = _conformer(u, dict(cf_w=cf_conv_w[l], cf_b=cf_conv_b[l, None, :],
                                 cf_lnw=cf_ln_w[l, None, :], cf_lnb=cf_ln_b[l, None, :]),
                         bsz, s_len, tl["t_conv"])
        merged = _merge(h, (y_a, y_b, y_c, y_d), wg[l], b_gate[l], wb[l], tl["tm_merge"], tl["tn_merge"])
        last = l == depth - 1
        nw = final_norm_w if last else norm_w[l + 1]
        outs = _out_proj(x2d, merged, wo[l], nw, tl["tm_out"], last)
        if last:
            return outs[0].reshape(bsz, s_len, d)
        x2d, h = outs
```

```python
import functools

import jax
import jax.numpy as jnp
from jax import lax
from jax.experimental import pallas as pl
from jax.experimental.pallas import tpu as pltpu

F32 = jnp.float32
BF16 = jnp.bfloat16

D_MODEL = 2048
EPS = 1e-6
NEG_INF = -1e30
LOG2E = 1.4426950408889634

FOX_HEADS = 8
FOX_HEAD_DIM = 128
FOX_AUG = 2 * FOX_HEAD_DIM
SSM_HEADS = 16
SSM_HEAD_DIM = 64
SSM_GROUPS = 2
SSM_STATE = 128
SSM_CONV = 4
SC_CONV = 3
CF_CONV = 31
WIDTH = 1024
N_BRANCH = 4

COL_Q, COL_K, COL_V, COL_GA, COL_Z = 0, 1, 2, 3, 4
COL_SCB, COL_SCC, COL_SCX, COL_GC = 5, 6, 7, 8
COL_GLUA, COL_GLUG, COL_GD, COL_XS = 9, 10, 11, 12
COL_BC_512 = 26
N_MAIN = 13 * WIDTH + 512
LANE_F = 0
LANE_DT = 8
SMALL = 128

SUBLANES = 8
HALO = SUBLANES
CF_HALO = 32

V7X_VMEM_LIMIT = 56 * 1024 * 1024


def _cparams(sem):
    return pltpu.CompilerParams(dimension_semantics=sem, vmem_limit_bytes=V7X_VMEM_LIMIT)


def _sigmoid(x):
    return 1.0 / (1.0 + jnp.exp(-x))


def _silu(x):
    return x * _sigmoid(x)


def _softplus(x):
    return jnp.maximum(x, 0.0) + jnp.log1p(jnp.exp(-jnp.abs(x)))


def _split3(x):
    hi = x.astype(BF16)
    r1 = x - hi.astype(F32)
    mid = r1.astype(BF16)
    lo = (r1 - mid.astype(F32)).astype(BF16)
    return hi, mid, lo


def _exact_dot(m_bf16, x):
    hi, mid, lo = _split3(x)
    d = lambda p: jnp.dot(m_bf16, p, preferred_element_type=F32)
    return d(hi) + d(mid) + d(lo)


def _exact_dot_r(x, m_bf16):
    hi, mid, lo = _split3(x)
    d = lambda p: jnp.dot(p, m_bf16, preferred_element_type=F32)
    return d(hi) + d(mid) + d(lo)


def _tril_ones(n):
    r = lax.broadcasted_iota(jnp.int32, (n, n), 0)
    c = lax.broadcasted_iota(jnp.int32, (n, n), 1)
    return r >= c


def _rmsnorm_kernel(x_ref, w_ref, h_ref):
    x = x_ref[...]
    y = x * lax.rsqrt(jnp.mean(x * x, axis=-1, keepdims=True) + EPS)
    h_ref[...] = (y * w_ref[...]).astype(h_ref.dtype)


def _rmsnorm(x2d, w, tm):
    m, d = x2d.shape
    return pl.pallas_call(
        _rmsnorm_kernel,
        grid=(m // tm,),
        in_specs=[pl.BlockSpec((tm, d), lambda i: (i, 0)),
                  pl.BlockSpec((1, d), lambda i: (0, 0))],
        out_specs=pl.BlockSpec((tm, d), lambda i: (i, 0)),
        out_shape=jax.ShapeDtypeStruct((m, d), BF16),
        compiler_params=_cparams(("parallel",)),
        name="rmsnorm",
    )(x2d, w.reshape(1, d))


def _matmul_nt_kernel(h_ref, wt_ref, o_ref):
    o_ref[...] = lax.dot_general(h_ref[...], wt_ref[...], (((1,), (1,)), ((), ())),
                                 preferred_element_type=F32).astype(o_ref.dtype)


def _matmul_nt(h, wt, layer, tm, tn, out_dtype, name):
    m, k = h.shape
    n = wt.shape[1]
    return pl.pallas_call(
        _matmul_nt_kernel,
        grid=(m // tm, n // tn),
        in_specs=[pl.BlockSpec((tm, k), lambda i, j: (i, 0)),
                  pl.BlockSpec((None, tn, k), lambda i, j: (layer, j, 0))],
        out_specs=pl.BlockSpec((tm, tn), lambda i, j: (i, j)),
        out_shape=jax.ShapeDtypeStruct((m, n), out_dtype),
        compiler_params=_cparams(("parallel", "arbitrary")),
        name=name,
    )(h, wt)


def _fox_prep_kernel(small_ref, bias_ref, q_ref, k_ref, v_ref, qt_ref, kp_ref, vt_ref, carry_ref):
    ts = small_ref.shape[0]

    @pl.when(pl.program_id(1) == 0)
    def _():
        carry_ref[...] = jnp.zeros_like(carry_ref)

    f = small_ref[...] + bias_ref[...]
    logf = -_softplus(-f)
    tri = _tril_ones(ts).astype(BF16)
    c = _exact_dot(tri, logf) + carry_ref[0:1, :]
    carry_ref[...] = jnp.broadcast_to(c[ts - 1:ts, :], carry_ref.shape)
    c = c * LOG2E

    def split(x):
        hi = x.astype(BF16).astype(F32)
        r1 = x - hi
        mid = r1.astype(BF16).astype(F32)
        return hi, mid, (r1 - mid).astype(BF16).astype(F32)

    c_t = c.T
    lane = lax.broadcasted_iota(jnp.int32, (ts, FOX_HEAD_DIM), 1)
    sub = lax.broadcasted_iota(jnp.int32, (FOX_HEAD_DIM, ts), 0)
    ones_q = jnp.where(sub < 6, 1.0, 0.0)
    ones_k = jnp.where(lane < 3, 1.0, 0.0)
    for h in range(FOX_HEADS):
        hi, mid, lo = split(c_t[LANE_F + h:LANE_F + h + 1, :])
        aq_t = jnp.where(sub == 0, hi, jnp.where(sub == 1, mid, jnp.where(sub == 2, lo, ones_q)))
        hi, mid, lo = split(c[:, LANE_F + h:LANE_F + h + 1])
        ak = jnp.where(lane == 3, -hi, jnp.where(lane == 4, -mid, jnp.where(lane == 5, -lo, ones_k)))
        sl = slice(h * FOX_HEAD_DIM, (h + 1) * FOX_HEAD_DIM)
        qt_ref[h, 0:FOX_HEAD_DIM, :] = q_ref[:, sl].astype(F32).T.astype(BF16)
        qt_ref[h, FOX_HEAD_DIM:FOX_AUG, :] = aq_t.astype(BF16)
        kp_ref[h, :, 0:FOX_HEAD_DIM] = k_ref[:, sl]
        kp_ref[h, :, FOX_HEAD_DIM:FOX_AUG] = ak.astype(BF16)
        vt_ref[h] = v_ref[:, sl].astype(F32).T.astype(BF16)


def _fox_prep(u, small, bias_row, bsz, s_len, ts):
    nt = s_len // ts
    row = lambda b, i: b * nt + i
    out = jax.ShapeDtypeStruct((bsz, FOX_HEADS, s_len, FOX_AUG), BF16)
    ospec = pl.BlockSpec((None, FOX_HEADS, ts, FOX_AUG), lambda b, i: (b, 0, i, 0))
    out_t = jax.ShapeDtypeStruct((bsz, FOX_HEADS, FOX_AUG, s_len), BF16)
    ospec_t = pl.BlockSpec((None, FOX_HEADS, FOX_AUG, ts), lambda b, i: (b, 0, 0, i))
    return pl.pallas_call(
        _fox_prep_kernel,
        grid=(bsz, nt),
        in_specs=[pl.BlockSpec((ts, SMALL), lambda b, i: (row(b, i), 0)),
                  pl.BlockSpec((1, SMALL), lambda b, i: (0, 0)),
                  pl.BlockSpec((ts, WIDTH), lambda b, i: (row(b, i), COL_Q)),
                  pl.BlockSpec((ts, WIDTH), lambda b, i: (row(b, i), COL_K)),
                  pl.BlockSpec((ts, WIDTH), lambda b, i: (row(b, i), COL_V))],
        out_specs=[ospec_t, ospec,
                   pl.BlockSpec((None, FOX_HEADS, None, FOX_HEAD_DIM, ts), lambda b, i: (b, 0, i, 0, 0))],
        out_shape=[out_t, out,
                   jax.ShapeDtypeStruct((bsz, FOX_HEADS, nt, FOX_HEAD_DIM, ts), BF16)],
        scratch_shapes=[pltpu.VMEM((8, SMALL), F32)],
        compiler_params=_cparams(("parallel", "arbitrary")),
        name="fox_prep",
    )(small, bias_row, u, u, u)


def _flash_kernel(qt_ref, k_ref, vt_ref, g_ref, o_ref, acc_ref, s0_ref, s1_ref):
    tq = qt_ref.shape[1]
    tk = s0_ref.shape[0]
    n = 2 * pl.program_id(2)

    def scores(kj, s_ref, q_lo=0):
        k_blk = k_ref[pl.ds(pl.multiple_of(kj * tk, tk), tk), :]
        s_ref[:, q_lo:] = jnp.dot(k_blk, qt_ref[:, q_lo:], preferred_element_type=F32)

    def softmax_pv(kj, s_ref, m, l, diag=None, q_lo=0):
        s = s_ref[:, q_lo:]
        if diag is not None:
            key = lax.broadcasted_iota(jnp.int32, s.shape, 0) + diag * tk
            qry = lax.broadcasted_iota(jnp.int32, s.shape, 1) + q_lo
            s = jnp.where(key <= qry, s, NEG_INF)
        m_old, l_old = m[:, q_lo:], l[:, q_lo:]
        m_new = jnp.maximum(m_old, jnp.max(s, axis=0, keepdims=True))
        alpha = jnp.exp2(m_old - m_new)
        p = jnp.exp2(s - m_new)
        l_new = alpha * l_old + jnp.sum(p, axis=0, keepdims=True)
        acc_ref[:, q_lo:] = alpha * acc_ref[:, q_lo:] + jnp.dot(vt_ref[kj], p.astype(BF16),
                                                                preferred_element_type=F32)
        if q_lo:
            m_new = jnp.concatenate([m[:, :q_lo], m_new], axis=1)
            l_new = jnp.concatenate([l[:, :q_lo], l_new], axis=1)
        return m_new, l_new

    acc_ref[...] = jnp.zeros_like(acc_ref)
    m0 = jnp.full((1, tq), -jnp.inf, F32)
    l0 = jnp.zeros((1, tq), F32)
    scores(0, s0_ref)

    def pair(p, c):
        m, l = c
        scores(2 * p + 1, s1_ref)
        m, l = softmax_pv(2 * p, s0_ref, m, l)
        scores(2 * p + 2, s0_ref)
        return softmax_pv(2 * p + 1, s1_ref, m, l)

    m, l = lax.fori_loop(0, n // 2, pair, (m0, l0))
    scores(n + 1, s1_ref, q_lo=tk)
    m, l = softmax_pv(n, s0_ref, m, l, diag=0)
    m, l = softmax_pv(n + 1, s1_ref, m, l, diag=1, q_lo=tk)
    o = (acc_ref[...] / l).T
    o_ref[...] = (o * _silu(g_ref[...].astype(F32))).astype(o_ref.dtype)


def _flash(qt, kp, vt, u, bsz, s_len, tk):
    tq = 2 * tk
    nq = s_len // tq
    return pl.pallas_call(
        _flash_kernel,
        grid=(bsz, FOX_HEADS, nq),
        in_specs=[
            pl.BlockSpec((None, None, FOX_AUG, tq), lambda b, h, qi: (b, h, 0, qi)),
            pl.BlockSpec((None, None, s_len, FOX_AUG), lambda b, h, qi: (b, h, 0, 0)),
            pl.BlockSpec((None, None, s_len // tk, FOX_HEAD_DIM, tk), lambda b, h, qi: (b, h, 0, 0, 0)),
            pl.BlockSpec((tq, FOX_HEAD_DIM), lambda b, h, qi: (b * nq + qi, COL_GA * FOX_HEADS + h)),
        ],
        out_specs=pl.BlockSpec((tq, FOX_HEAD_DIM), lambda b, h, qi: (b * nq + qi, h)),
        out_shape=jax.ShapeDtypeStruct((bsz * s_len, WIDTH), BF16),
        scratch_shapes=[pltpu.VMEM((FOX_HEAD_DIM, tq), F32), pltpu.VMEM((tk, tq), F32),
                        pltpu.VMEM((tk, tq), F32)],
        compiler_params=_cparams(("parallel", "parallel", "arbitrary")),
        name="fox_flash",
    )(qt, kp, vt, u)


def _causal_conv(ext_ref, w_ref, b_ref, taps, halo, t):
    acc = None
    for k in range(taps):
        off = halo - (taps - 1) + k
        term = w_ref[k:k + 1, :] * ext_ref[off:off + t, :]
        acc = term if acc is None else acc + term
    return acc + b_ref[...]


def _causal_conv_long(ext_ref, shift_ref, w_ref, b_ref, taps, halo, t):
    acc = None
    for r in range(SUBLANES):
        ks = [k for k in range(taps) if (taps - 1 - k) % SUBLANES == r]
        if not ks:
            continue
        max_a = max((taps - 1 - k) // SUBLANES for k in ks)
        rows = t + SUBLANES * max_a
        start = halo - SUBLANES * max_a - r
        shift_ref[0:rows, :] = ext_ref[start:start + rows, :]
        for k in ks:
            off = SUBLANES * (max_a - (taps - 1 - k) // SUBLANES)
            term = w_ref[k:k + 1, :] * shift_ref[off:off + t, :]
            acc = term if acc is None else acc + term
    return acc + b_ref[...]


def _ssd_kernel(xs_ref, bc_ref, z_ref, small_ref, cwx_ref, cbx_ref, cwbc_ref, cbbc_ref,
                dtb_ref, alog_ref, dskip_ref, nw_ref, o_ref,
                extx_ref, extbc_ref, state_ref):
    t = xs_ref.shape[0]
    hp = SSM_HEADS // SSM_GROUPS * SSM_HEAD_DIM

    @pl.when(pl.program_id(1) == 0)
    def _():
        extx_ref[0:HALO, :] = jnp.zeros((HALO, WIDTH), F32)
        extbc_ref[0:HALO, :] = jnp.zeros((HALO, 512), F32)
        state_ref[...] = jnp.zeros_like(state_ref)

    extx_ref[HALO:HALO + t, :] = xs_ref[...].astype(F32)
    extbc_ref[HALO:HALO + t, :] = bc_ref[...].astype(F32)
    xs = _silu(_causal_conv(extx_ref, cwx_ref, cbx_ref, SSM_CONV, HALO, t))
    bc = _silu(_causal_conv(extbc_ref, cwbc_ref, cbbc_ref, SSM_CONV, HALO, t))
    extx_ref[0:HALO, :] = extx_ref[t:t + HALO, :]
    extbc_ref[0:HALO, :] = extbc_ref[t:t + HALO, :]

    dt = _softplus(small_ref[...] + dtb_ref[...])
    da = dt * (-jnp.exp(alog_ref[...]))
    tri = _tril_ones(t)
    cs = _exact_dot(tri.astype(BF16), da)
    er = lax.broadcasted_iota(jnp.int32, (SMALL, WIDTH), 0)
    ec = lax.broadcasted_iota(jnp.int32, (SMALL, WIDTH), 1)
    expand = (er == LANE_DT + ec // SSM_HEAD_DIM).astype(BF16)
    dt_e = _exact_dot_r(dt, expand)
    cs_e = _exact_dot_r(cs, expand)
    cs_last_e = cs_e[t - 1:t, :]

    xdt = xs * dt_e
    xdt_b = xdt.astype(BF16)
    xdec_b = (xdt * jnp.exp(cs_last_e - cs_e)).astype(BF16)
    cs_t = cs.T
    lane = lax.broadcasted_iota(jnp.int32, (t, 2 * SSM_HEAD_DIM), 1)
    lo_half = lane < SSM_HEAD_DIM

    y_parts = []
    for g in range(SSM_GROUPS):
        b_g = bc[:, g * SSM_STATE:(g + 1) * SSM_STATE].astype(BF16)
        c_g = bc[:, (SSM_GROUPS + g) * SSM_STATE:(SSM_GROUPS + g + 1) * SSM_STATE].astype(BF16)
        cb = lax.dot_general(c_g, b_g, (((1,), (1,)), ((), ())), preferred_element_type=F32)
        for pair in range(SSM_HEADS // SSM_GROUPS // 2):
            h0 = g * (SSM_HEADS // SSM_GROUPS) + 2 * pair
            gs = []
            for h in (h0, h0 + 1):
                seg = cs[:, LANE_DT + h:LANE_DT + h + 1] - cs_t[LANE_DT + h:LANE_DT + h + 1, :]
                lmat = jnp.exp(jnp.where(tri, seg, -jnp.inf))
                gs.append((cb * lmat).astype(BF16))
            xp = xdt_b[:, h0 * SSM_HEAD_DIM:(h0 + 2) * SSM_HEAD_DIM]
            zero = jnp.zeros_like(xp)
            rhs = jnp.concatenate([jnp.where(lo_half, xp, zero), jnp.where(lo_half, zero, xp)], axis=0)
            y_parts.append(jnp.dot(jnp.concatenate(gs, axis=1), rhs, preferred_element_type=F32))
    y_diag = jnp.concatenate(y_parts, axis=1)

    y_off_parts = []
    for g in range(SSM_GROUPS):
        b_g = bc[:, g * SSM_STATE:(g + 1) * SSM_STATE].astype(BF16)
        c_g = bc[:, (SSM_GROUPS + g) * SSM_STATE:(SSM_GROUPS + g + 1) * SSM_STATE].astype(BF16)
        prev = state_ref[g]
        y_off_parts.append(jnp.dot(c_g, prev.astype(BF16), preferred_element_type=F32))
        upd = lax.dot_general(b_g, xdec_b[:, g * hp:(g + 1) * hp], (((0,), (0,)), ((), ())),
                              preferred_element_type=F32)
        state_ref[g] = prev * jnp.exp(cs_last_e[:, g * hp:(g + 1) * hp]) + upd
    y_off = jnp.concatenate(y_off_parts, axis=1) * jnp.exp(cs_e)

    y = y_diag + y_off + xs * dskip_ref[...]
    v = y * _silu(z_ref[...].astype(F32))
    v = v * lax.rsqrt(jnp.mean(v * v, axis=-1, keepdims=True) + EPS)
    o_ref[...] = (v * nw_ref[...]).astype(o_ref.dtype)


def _ssd(u, small, p, bsz, s_len, t):
    nt = s_len // t
    row = lambda b, i: b * nt + i
    full = lambda shape: pl.BlockSpec(shape, lambda b, i: (0, 0))
    return pl.pallas_call(
        _ssd_kernel,
        grid=(bsz, nt),
        in_specs=[pl.BlockSpec((t, WIDTH), lambda b, i: (row(b, i), COL_XS)),
                  pl.BlockSpec((t, 512), lambda b, i: (row(b, i), COL_BC_512)),
                  pl.BlockSpec((t, WIDTH), lambda b, i: (row(b, i), COL_Z)),
                  pl.BlockSpec((t, SMALL), lambda b, i: (row(b, i), 0)),
                  full((SSM_CONV, WIDTH)), full((1, WIDTH)),
                  full((SSM_CONV, 512)), full((1, 512)),
                  full((1, SMALL)), full((1, SMALL)), full((1, WIDTH)), full((1, WIDTH))],
        out_specs=pl.BlockSpec((t, WIDTH), lambda b, i: (row(b, i), 0)),
        out_shape=jax.ShapeDtypeStruct((bsz * s_len, WIDTH), BF16),
        scratch_shapes=[pltpu.VMEM((HALO + t, WIDTH), F32),
                        pltpu.VMEM((HALO + t, 512), F32),
                        pltpu.VMEM((SSM_GROUPS, SSM_STATE, 512), F32)],
        compiler_params=_cparams(("parallel", "arbitrary")),
        name="ssd",
    )(u, u, u, small, p["cwx"], p["cbx"], p["cwbc"], p["cbbc"],
      p["dtb"], p["alog"], p["dskip"], p["ssm_nw"])


def _shortconv_kernel(b_ref, c_ref, x_ref, g_ref, w_ref, cb_ref, o_ref, ext_ref):
    t = x_ref.shape[0]

    @pl.when(pl.program_id(1) == 0)
    def _():
        ext_ref[0:HALO, :] = jnp.zeros((HALO, WIDTH), F32)

    ext_ref[HALO:HALO + t, :] = c_ref[...].astype(F32) * x_ref[...].astype(F32)
    conv = _causal_conv(ext_ref, w_ref, cb_ref, SC_CONV, HALO, t)
    ext_ref[0:HALO, :] = ext_ref[t:t + HALO, :]
    o_ref[...] = (b_ref[...].astype(F32) * conv * _silu(g_ref[...].astype(F32))).astype(o_ref.dtype)


def _shortconv(u, p, bsz, s_len, t):
    nt = s_len // t
    row = lambda b, i: b * nt + i
    col = lambda c: pl.BlockSpec((t, WIDTH), lambda b, i: (row(b, i), c))
    full = lambda shape: pl.BlockSpec(shape, lambda b, i: (0, 0))
    return pl.pallas_call(
        _shortconv_kernel,
        grid=(bsz, nt),
        in_specs=[col(COL_SCB), col(COL_SCC), col(COL_SCX), col(COL_GC),
                  full((SC_CONV, WIDTH)), full((1, WIDTH))],
        out_specs=pl.BlockSpec((t, WIDTH), lambda b, i: (row(b, i), 0)),
        out_shape=jax.ShapeDtypeStruct((bsz * s_len, WIDTH), BF16),
        scratch_shapes=[pltpu.VMEM((HALO + t, WIDTH), F32)],
        compiler_params=_cparams(("parallel", "arbitrary")),
        name="shortconv",
    )(u, u, u, u, p["sc_w"], p["sc_b"])


def _conformer_kernel(a_ref, gg_ref, g_ref, w_ref, cb_ref, lnw_ref, lnb_ref, o_ref, ext_ref, shift_ref):
    t = a_ref.shape[0]

    @pl.when(pl.program_id(1) == 0)
    def _():
        ext_ref[0:CF_HALO, :] = jnp.zeros((CF_HALO, WIDTH), F32)

    ext_ref[CF_HALO:CF_HALO + t, :] = a_ref[...].astype(F32) * _sigmoid(gg_ref[...].astype(F32))
    cf = _causal_conv_long(ext_ref, shift_ref, w_ref, cb_ref, CF_CONV, CF_HALO, t)
    ext_ref[0:CF_HALO, :] = ext_ref[t:t + CF_HALO, :]
    mu = jnp.mean(cf, axis=-1, keepdims=True)
    d = cf - mu
    var = jnp.mean(d * d, axis=-1, keepdims=True)
    ln = d * lax.rsqrt(var + EPS) * lnw_ref[...] + lnb_ref[...]
    o_ref[...] = (_silu(ln) * _silu(g_ref[...].astype(F32))).astype(o_ref.dtype)


def _conformer(u, p, bsz, s_len, t):
    nt = s_len // t
    row = lambda b, i: b * nt + i
    col = lambda c: pl.BlockSpec((t, WIDTH), lambda b, i: (row(b, i), c))
    full = lambda shape: pl.BlockSpec(shape, lambda b, i: (0, 0))
    return pl.pallas_call(
        _conformer_kernel,
        grid=(bsz, nt),
        in_specs=[col(COL_GLUA), col(COL_GLUG), col(COL_GD),
                  full((CF_CONV, WIDTH)), full((1, WIDTH)), full((1, WIDTH)), full((1, WIDTH))],
        out_specs=pl.BlockSpec((t, WIDTH), lambda b, i: (row(b, i), 0)),
        out_shape=jax.ShapeDtypeStruct((bsz * s_len, WIDTH), BF16),
        scratch_shapes=[pltpu.VMEM((CF_HALO + t, WIDTH), F32), pltpu.VMEM((CF_HALO + t, WIDTH), F32)],
        compiler_params=_cparams(("parallel", "arbitrary")),
        name="conformer",
    )(u, u, u, p["cf_w"], p["cf_b"], p["cf_lnw"], p["cf_lnb"])


def _merge_kernel(h_ref, ya_ref, yb_ref, yc_ref, yd_ref, wg_ref, bg_ref, wb_ref, o_ref):
    h = h_ref[...]
    acc = None
    for i, y_ref in enumerate((ya_ref, yb_ref, yc_ref, yd_ref)):
        gate = _sigmoid(jnp.dot(h, wg_ref[i], preferred_element_type=F32) + bg_ref[i:i + 1, :])
        term = gate * jnp.dot(y_ref[...], wb_ref[i], preferred_element_type=F32)
        acc = term if acc is None else acc + term
    o_ref[...] = acc.astype(o_ref.dtype)


def _merge(h, ys, wg, bg, wb, layer, tm, tn):
    m = h.shape[0]
    yspec = pl.BlockSpec((tm, WIDTH), lambda i, j: (i, 0))
    return pl.pallas_call(
        _merge_kernel,
        grid=(m // tm, D_MODEL // tn),
        in_specs=[pl.BlockSpec((tm, D_MODEL), lambda i, j: (i, 0)),
                  yspec, yspec, yspec, yspec,
                  pl.BlockSpec((None, N_BRANCH, D_MODEL, tn), lambda i, j: (layer, 0, 0, j)),
                  pl.BlockSpec((None, N_BRANCH, tn), lambda i, j: (layer, 0, j)),
                  pl.BlockSpec((None, N_BRANCH, WIDTH, tn), lambda i, j: (layer, 0, 0, j))],
        out_specs=pl.BlockSpec((tm, tn), lambda i, j: (i, j)),
        out_shape=jax.ShapeDtypeStruct((m, D_MODEL), BF16),
        compiler_params=_cparams(("parallel", "arbitrary")),
        name="merge",
    )(h, *ys, wg, bg, wb)


def _out_kernel(x_ref, mg_ref, w_ref, nw_ref, *out_refs, last):
    x_new = x_ref[...] + jnp.dot(mg_ref[...], w_ref[...], preferred_element_type=F32)
    normed = x_new * lax.rsqrt(jnp.mean(x_new * x_new, axis=-1, keepdims=True) + EPS) * nw_ref[...]
    if last:
        out_refs[0][...] = normed.astype(out_refs[0].dtype)
    else:
        out_refs[0][...] = x_new
        out_refs[1][...] = normed.astype(out_refs[1].dtype)


def _out_proj(x2d, merged, w_out, layer, next_norm_w, tm, last):
    m, d = x2d.shape
    row = pl.BlockSpec((tm, d), lambda i: (i, 0))
    if last:
        out_shape = [jax.ShapeDtypeStruct((m, d), F32)]
    else:
        out_shape = [jax.ShapeDtypeStruct((m, d), F32), jax.ShapeDtypeStruct((m, d), BF16)]
    return pl.pallas_call(
        functools.partial(_out_kernel, last=last),
        grid=(m // tm,),
        in_specs=[row, row,
                  pl.BlockSpec((None, d, d), lambda i: (layer, 0, 0)),
                  pl.BlockSpec((1, d), lambda i: (0, 0))],
        out_specs=[row] * len(out_shape),
        out_shape=out_shape,
        compiler_params=_cparams(("parallel",)),
        name="out_proj_last" if last else "out_proj",
    )(x2d, merged, w_out, next_norm_w.reshape(1, d))


def _pad_lanes(v, start):
    n = v.shape[-1]
    return jnp.pad(v.astype(F32), ((0, 0), (start, SMALL - start - n)))[:, None, :]


def _w_relayout_kernel(wt_ref, main_ref, small_ref):
    tc = wt_ref.shape[1]
    scale = FOX_HEAD_DIM ** -0.5 * LOG2E
    f0 = 3 * WIDTH
    xbc0 = f0 + FOX_HEADS + 2 * WIDTH
    dt0 = xbc0 + WIDTH + 512
    rest0 = dt0 + SSM_HEADS
    main_ref[0:WIDTH, :] = (wt_ref[0:WIDTH, :] * scale).astype(BF16)
    main_ref[WIDTH:f0, :] = wt_ref[WIDTH:f0, :].astype(BF16)
    main_ref[f0:f0 + 2 * WIDTH, :] = wt_ref[f0 + FOX_HEADS:xbc0, :].astype(BF16)
    main_ref[5 * WIDTH:12 * WIDTH, :] = wt_ref[rest0:rest0 + 7 * WIDTH, :].astype(BF16)
    main_ref[12 * WIDTH:N_MAIN, :] = wt_ref[xbc0:dt0, :].astype(BF16)
    pad = jnp.zeros((SMALL - FOX_HEADS - SSM_HEADS, tc), F32)
    small_ref[...] = jnp.concatenate([wt_ref[f0:f0 + FOX_HEADS, :], wt_ref[dt0:rest0, :], pad],
                                     axis=0).astype(BF16)


def _w_relayout(w_in, tc):
    wt = jnp.swapaxes(w_in, 1, 2)
    depth, n_in, d = wt.shape
    return pl.pallas_call(
        _w_relayout_kernel,
        grid=(depth, d // tc),
        in_specs=[pl.BlockSpec((None, n_in, tc), lambda l, i: (l, 0, i))],
        out_specs=[pl.BlockSpec((None, N_MAIN, tc), lambda l, i: (l, 0, i)),
                   pl.BlockSpec((None, SMALL, tc), lambda l, i: (l, 0, i))],
        out_shape=[jax.ShapeDtypeStruct((depth, N_MAIN, d), BF16),
                   jax.ShapeDtypeStruct((depth, SMALL, d), BF16)],
        compiler_params=_cparams(("parallel", "parallel")),
        name="w_relayout",
    )(wt)


def _tiles(m, s_len):
    pick = lambda want, n: min(want, n)
    return dict(t_relayout=256, tm_in=pick(1024, m), tn_in=1536, tm_norm=pick(512, m),
                t_attn=pick(512, s_len), t_ssd=pick(128, s_len), t_conv=pick(256, s_len),
                tm_merge=pick(512, m), tn_merge=512, tm_out=pick(512, m))


def kernel(x, norm_w, w_in, fg_bias, ssm_conv_w, ssm_conv_b, dt_bias, a_log, d_skip, ssm_norm_w,
           sc_conv_w, sc_conv_b, cf_conv_w, cf_conv_b, cf_ln_w, cf_ln_b, w_gate, b_gate, w_branch,
           w_out, final_norm_w):
    bsz, s_len, d = x.shape
    depth = w_in.shape[0]
    m = bsz * s_len
    tl = _tiles(m, s_len)

    w_main, w_small = _w_relayout(w_in, tl["t_relayout"])
    wg = w_gate.astype(BF16)
    wb = w_branch.astype(BF16)
    wo = w_out.astype(BF16)
    fg_row = _pad_lanes(fg_bias, LANE_F)
    dtb_row = _pad_lanes(dt_bias, LANE_DT)
    alog_row = _pad_lanes(a_log, LANE_DT)
    dskip_row = jnp.repeat(d_skip.astype(F32), SSM_HEAD_DIM, axis=-1)[:, None, :]
    row = lambda a: a.astype(F32)[:, None, :]

    x2d = x.reshape(m, d)
    h = _rmsnorm(x2d, norm_w[0], tl["tm_norm"])
    for l in range(depth):
        u = _matmul_nt(h, w_main, l, tl["tm_in"], tl["tn_in"], BF16, "in_proj")
        small = _matmul_nt(h, w_small, l, tl["tm_in"], SMALL, F32, "in_proj_small")
        qp, kp, vt = _fox_prep(u, small, fg_row[l], bsz, s_len, tl["t_attn"])
        y_a = _flash(qp, kp, vt, u, bsz, s_len, tl["t_attn"])
        ssd_p = dict(cwx=ssm_conv_w[l, :, :WIDTH], cbx=ssm_conv_b[l, None, :WIDTH],
                     cwbc=ssm_conv_w[l, :, WIDTH:], cbbc=ssm_conv_b[l, None, WIDTH:],
                     dtb=dtb_row[l], alog=alog_row[l], dskip=dskip_row[l], ssm_nw=row(ssm_norm_w)[l])
        y_b = _ssd(u, small, ssd_p, bsz, s_len, tl["t_ssd"])
        y_c = _shortconv(u, dict(sc_w=sc_conv_w[l], sc_b=sc_conv_b[l, None, :]), bsz, s_len, tl["t_conv"])
        y_d = _conformer(u, dict(cf_w=cf_conv_w[l], cf_b=cf_conv_b[l, None, :],
                                 cf_lnw=cf_ln_w[l, None, :], cf_lnb=cf_ln_b[l, None, :]),
                         bsz, s_len, tl["t_conv"])
        merged = _merge(h, (y_a, y_b, y_c, y_d), wg, b_gate, wb, l, tl["tm_merge"], tl["tn_merge"])
        last = l == depth - 1
        nw = final_norm_w if last else norm_w[l + 1]
        outs = _out_proj(x2d, merged, wo, l, nw, tl["tm_out"], last)
        if last:
            return outs[0].reshape(bsz, s_len, d)
        x2d, h = outs
```

```python
import functools

import jax
import jax.numpy as jnp
from jax import lax
from jax.experimental import pallas as pl
from jax.experimental.pallas import tpu as pltpu

F32 = jnp.float32
BF16 = jnp.bfloat16

D_MODEL = 2048
EPS = 1e-6
NEG_INF = -1e30
LOG2E = 1.4426950408889634

FOX_HEADS = 8
FOX_HEAD_DIM = 128
FOX_AUG = 2 * FOX_HEAD_DIM
SSM_HEADS = 16
SSM_HEAD_DIM = 64
SSM_GROUPS = 2
SSM_STATE = 128
SSM_CONV = 4
SC_CONV = 3
CF_CONV = 31
WIDTH = 1024
N_BRANCH = 4

COL_Q, COL_K, COL_V, COL_GA, COL_Z = 0, 1, 2, 3, 4
COL_SCB, COL_SCC, COL_SCX, COL_GC = 5, 6, 7, 8
COL_GLUA, COL_GLUG, COL_GD, COL_XS = 9, 10, 11, 12
COL_BC_512 = 26
N_MAIN = 13 * WIDTH + 512
LANE_F = 0
LANE_DT = 8
SMALL = 128

SUBLANES = 8
HALO = SUBLANES
CF_HALO = 32

V7X_VMEM_LIMIT = 56 * 1024 * 1024


def _cparams(sem):
    return pltpu.CompilerParams(dimension_semantics=sem, vmem_limit_bytes=V7X_VMEM_LIMIT)


def _sigmoid(x):
    return 1.0 / (1.0 + jnp.exp(-x))


def _silu(x):
    return x * _sigmoid(x)


def _softplus(x):
    return jnp.maximum(x, 0.0) + jnp.log1p(jnp.exp(-jnp.abs(x)))


def _split3(x):
    hi = x.astype(BF16)
    r1 = x - hi.astype(F32)
    mid = r1.astype(BF16)
    lo = (r1 - mid.astype(F32)).astype(BF16)
    return hi, mid, lo


def _exact_dot(m_bf16, x):
    hi, mid, lo = _split3(x)
    d = lambda p: jnp.dot(m_bf16, p, preferred_element_type=F32)
    return d(hi) + d(mid) + d(lo)


def _exact_dot_r(x, m_bf16):
    hi, mid, lo = _split3(x)
    d = lambda p: jnp.dot(p, m_bf16, preferred_element_type=F32)
    return d(hi) + d(mid) + d(lo)


def _tril_ones(n):
    r = lax.broadcasted_iota(jnp.int32, (n, n), 0)
    c = lax.broadcasted_iota(jnp.int32, (n, n), 1)
    return r >= c


def _rmsnorm_kernel(x_ref, w_ref, h_ref):
    x = x_ref[...]
    y = x * lax.rsqrt(jnp.mean(x * x, axis=-1, keepdims=True) + EPS)
    h_ref[...] = (y * w_ref[...]).astype(h_ref.dtype)


def _rmsnorm(x2d, w, tm):
    m, d = x2d.shape
    return pl.pallas_call(
        _rmsnorm_kernel,
        grid=(m // tm,),
        in_specs=[pl.BlockSpec((tm, d), lambda i: (i, 0)),
                  pl.BlockSpec((1, d), lambda i: (0, 0))],
        out_specs=pl.BlockSpec((tm, d), lambda i: (i, 0)),
        out_shape=jax.ShapeDtypeStruct((m, d), BF16),
        compiler_params=_cparams(("parallel",)),
        name="rmsnorm",
    )(x2d, w.reshape(1, d))


def _in_proj_kernel(h_ref, wt_ref, ws_ref, o_ref, small_ref):
    nt = (((1,), (1,)), ((), ()))
    o_ref[...] = lax.dot_general(h_ref[...], wt_ref[...], nt,
                                 preferred_element_type=F32).astype(o_ref.dtype)

    @pl.when(pl.program_id(1) == 0)
    def _():
        small_ref[...] = lax.dot_general(h_ref[...], ws_ref[...], nt, preferred_element_type=F32)


def _in_proj(h, wt, ws, layer, tm, tn):
    m, k = h.shape
    n = wt.shape[1]
    return pl.pallas_call(
        _in_proj_kernel,
        grid=(m // tm, n // tn),
        in_specs=[pl.BlockSpec((tm, k), lambda i, j: (i, 0)),
                  pl.BlockSpec((None, tn, k), lambda i, j: (layer, j, 0)),
                  pl.BlockSpec((None, SMALL, k), lambda i, j: (layer, 0, 0))],
        out_specs=[pl.BlockSpec((tm, tn), lambda i, j: (i, j)),
                   pl.BlockSpec((tm, SMALL), lambda i, j: (i, 0))],
        out_shape=[jax.ShapeDtypeStruct((m, n), BF16), jax.ShapeDtypeStruct((m, SMALL), F32)],
        compiler_params=_cparams(("parallel", "arbitrary")),
        name="in_proj",
    )(h, wt, ws)


def _fox_prep_kernel(small_ref, bias_ref, q_ref, k_ref, v_ref, qt_ref, kp_ref, vt_ref, carry_ref):
    ts = small_ref.shape[0]

    @pl.when(pl.program_id(1) == 0)
    def _():
        carry_ref[...] = jnp.zeros_like(carry_ref)

    f = small_ref[...] + bias_ref[...]
    logf = -_softplus(-f)
    tri = _tril_ones(ts).astype(BF16)
    c = _exact_dot(tri, logf) + carry_ref[0:1, :]
    carry_ref[...] = jnp.broadcast_to(c[ts - 1:ts, :], carry_ref.shape)
    c = c * LOG2E

    def split(x):
        hi = x.astype(BF16).astype(F32)
        r1 = x - hi
        mid = r1.astype(BF16).astype(F32)
        return hi, mid, (r1 - mid).astype(BF16).astype(F32)

    c_t = c.T
    lane = lax.broadcasted_iota(jnp.int32, (ts, FOX_HEAD_DIM), 1)
    sub = lax.broadcasted_iota(jnp.int32, (FOX_HEAD_DIM, ts), 0)
    ones_q = jnp.where(sub < 6, 1.0, 0.0)
    ones_k = jnp.where(lane < 3, 1.0, 0.0)
    for h in range(FOX_HEADS):
        hi, mid, lo = split(c_t[LANE_F + h:LANE_F + h + 1, :])
        aq_t = jnp.where(sub == 0, hi, jnp.where(sub == 1, mid, jnp.where(sub == 2, lo, ones_q)))
        hi, mid, lo = split(c[:, LANE_F + h:LANE_F + h + 1])
        ak = jnp.where(lane == 3, -hi, jnp.where(lane == 4, -mid, jnp.where(lane == 5, -lo, ones_k)))
        sl = slice(h * FOX_HEAD_DIM, (h + 1) * FOX_HEAD_DIM)
        qt_ref[h, 0:FOX_HEAD_DIM, :] = q_ref[:, sl].astype(F32).T.astype(BF16)
        qt_ref[h, FOX_HEAD_DIM:FOX_AUG, :] = aq_t.astype(BF16)
        kp_ref[h, :, 0:FOX_HEAD_DIM] = k_ref[:, sl]
        kp_ref[h, :, FOX_HEAD_DIM:FOX_AUG] = ak.astype(BF16)
        vt_ref[h] = v_ref[:, sl].astype(F32).T.astype(BF16)


def _fox_prep(u, small, bias_row, bsz, s_len, ts):
    nt = s_len // ts
    row = lambda b, i: b * nt + i
    out = jax.ShapeDtypeStruct((bsz, FOX_HEADS, s_len, FOX_AUG), BF16)
    ospec = pl.BlockSpec((None, FOX_HEADS, ts, FOX_AUG), lambda b, i: (b, 0, i, 0))
    out_t = jax.ShapeDtypeStruct((bsz, FOX_HEADS, FOX_AUG, s_len), BF16)
    ospec_t = pl.BlockSpec((None, FOX_HEADS, FOX_AUG, ts), lambda b, i: (b, 0, 0, i))
    return pl.pallas_call(
        _fox_prep_kernel,
        grid=(bsz, nt),
        in_specs=[pl.BlockSpec((ts, SMALL), lambda b, i: (row(b, i), 0)),
                  pl.BlockSpec((1, SMALL), lambda b, i: (0, 0)),
                  pl.BlockSpec((ts, WIDTH), lambda b, i: (row(b, i), COL_Q)),
                  pl.BlockSpec((ts, WIDTH), lambda b, i: (row(b, i), COL_K)),
                  pl.BlockSpec((ts, WIDTH), lambda b, i: (row(b, i), COL_V))],
        out_specs=[ospec_t, ospec,
                   pl.BlockSpec((None, FOX_HEADS, None, FOX_HEAD_DIM, ts), lambda b, i: (b, 0, i, 0, 0))],
        out_shape=[out_t, out,
                   jax.ShapeDtypeStruct((bsz, FOX_HEADS, nt, FOX_HEAD_DIM, ts), BF16)],
        scratch_shapes=[pltpu.VMEM((8, SMALL), F32)],
        compiler_params=_cparams(("parallel", "arbitrary")),
        name="fox_prep",
    )(small, bias_row, u, u, u)


def _flash_kernel(qt_ref, k_ref, vt_ref, g_ref, o_ref, acc_ref, s0_ref, s1_ref):
    tq = qt_ref.shape[1]
    tk = s0_ref.shape[0]
    n = 2 * pl.program_id(2)

    def scores(kj, s_ref, q_lo=0):
        k_blk = k_ref[pl.ds(pl.multiple_of(kj * tk, tk), tk), :]
        s_ref[:, q_lo:] = jnp.dot(k_blk, qt_ref[:, q_lo:], preferred_element_type=F32)

    def softmax_pv(kj, s_ref, m, l, diag=None, q_lo=0):
        s = s_ref[:, q_lo:]
        if diag is not None:
            key = lax.broadcasted_iota(jnp.int32, s.shape, 0) + diag * tk
            qry = lax.broadcasted_iota(jnp.int32, s.shape, 1) + q_lo
            s = jnp.where(key <= qry, s, NEG_INF)
        m_old, l_old = m[:, q_lo:], l[:, q_lo:]
        m_new = jnp.maximum(m_old, jnp.max(s, axis=0, keepdims=True))
        alpha = jnp.exp2(m_old - m_new)
        p = jnp.exp2(s - m_new)
        l_new = alpha * l_old + jnp.sum(p, axis=0, keepdims=True)
        acc_ref[:, q_lo:] = alpha * acc_ref[:, q_lo:] + jnp.dot(vt_ref[kj], p.astype(BF16),
                                                                preferred_element_type=F32)
        if q_lo:
            m_new = jnp.concatenate([m[:, :q_lo], m_new], axis=1)
            l_new = jnp.concatenate([l[:, :q_lo], l_new], axis=1)
        return m_new, l_new

    acc_ref[...] = jnp.zeros_like(acc_ref)
    m0 = jnp.full((1, tq), -jnp.inf, F32)
    l0 = jnp.zeros((1, tq), F32)
    scores(0, s0_ref)

    def pair(p, c):
        m, l = c
        scores(2 * p + 1, s1_ref)
        m, l = softmax_pv(2 * p, s0_ref, m, l)
        scores(2 * p + 2, s0_ref)
        return softmax_pv(2 * p + 1, s1_ref, m, l)

    m, l = lax.fori_loop(0, n // 2, pair, (m0, l0))
    scores(n + 1, s1_ref, q_lo=tk)
    m, l = softmax_pv(n, s0_ref, m, l, diag=0)
    m, l = softmax_pv(n + 1, s1_ref, m, l, diag=1, q_lo=tk)
    o = (acc_ref[...] / l).T
    o_ref[...] = (o * _silu(g_ref[...].astype(F32))).astype(o_ref.dtype)


def _flash(qt, kp, vt, u, bsz, s_len, tk):
    tq = 2 * tk
    nq = s_len // tq
    return pl.pallas_call(
        _flash_kernel,
        grid=(bsz, FOX_HEADS, nq),
        in_specs=[
            pl.BlockSpec((None, None, FOX_AUG, tq), lambda b, h, qi: (b, h, 0, qi)),
            pl.BlockSpec((None, None, s_len, FOX_AUG), lambda b, h, qi: (b, h, 0, 0)),
            pl.BlockSpec((None, None, s_len // tk, FOX_HEAD_DIM, tk), lambda b, h, qi: (b, h, 0, 0, 0)),
            pl.BlockSpec((tq, FOX_HEAD_DIM), lambda b, h, qi: (b * nq + qi, COL_GA * FOX_HEADS + h)),
        ],
        out_specs=pl.BlockSpec((tq, FOX_HEAD_DIM), lambda b, h, qi: (b * nq + qi, h)),
        out_shape=jax.ShapeDtypeStruct((bsz * s_len, WIDTH), BF16),
        scratch_shapes=[pltpu.VMEM((FOX_HEAD_DIM, tq), F32), pltpu.VMEM((tk, tq), F32),
                        pltpu.VMEM((tk, tq), F32)],
        compiler_params=_cparams(("parallel", "parallel", "arbitrary")),
        name="fox_flash",
    )(qt, kp, vt, u)


def _causal_conv(ext_ref, w_ref, b_ref, taps, halo, t):
    acc = None
    for k in range(taps):
        off = halo - (taps - 1) + k
        term = w_ref[k:k + 1, :] * ext_ref[off:off + t, :]
        acc = term if acc is None else acc + term
    return acc + b_ref[...]


def _causal_conv_long(ext_ref, shift_ref, w_ref, b_ref, taps, halo, t):
    acc = None
    for r in range(SUBLANES):
        ks = [k for k in range(taps) if (taps - 1 - k) % SUBLANES == r]
        if not ks:
            continue
        max_a = max((taps - 1 - k) // SUBLANES for k in ks)
        rows = t + SUBLANES * max_a
        start = halo - SUBLANES * max_a - r
        shift_ref[0:rows, :] = ext_ref[start:start + rows, :]
        for k in ks:
            off = SUBLANES * (max_a - (taps - 1 - k) // SUBLANES)
            term = w_ref[k:k + 1, :] * shift_ref[off:off + t, :]
            acc = term if acc is None else acc + term
    return acc + b_ref[...]


def _ssd_kernel(xs_ref, bc_ref, z_ref, small_ref, cwx_ref, cbx_ref, cwbc_ref, cbbc_ref,
                dtb_ref, alog_ref, dskip_ref, nw_ref, o_ref,
                extx_ref, extbc_ref, state_ref):
    t = xs_ref.shape[0]
    hp = SSM_HEADS // SSM_GROUPS * SSM_HEAD_DIM

    @pl.when(pl.program_id(1) == 0)
    def _():
        extx_ref[0:HALO, :] = jnp.zeros((HALO, WIDTH), F32)
        extbc_ref[0:HALO, :] = jnp.zeros((HALO, 512), F32)
        state_ref[...] = jnp.zeros_like(state_ref)

    extx_ref[HALO:HALO + t, :] = xs_ref[...].astype(F32)
    extbc_ref[HALO:HALO + t, :] = bc_ref[...].astype(F32)
    xs = _silu(_causal_conv(extx_ref, cwx_ref, cbx_ref, SSM_CONV, HALO, t))
    bc = _silu(_causal_conv(extbc_ref, cwbc_ref, cbbc_ref, SSM_CONV, HALO, t))
    extx_ref[0:HALO, :] = extx_ref[t:t + HALO, :]
    extbc_ref[0:HALO, :] = extbc_ref[t:t + HALO, :]

    dt = _softplus(small_ref[...] + dtb_ref[...])
    da = dt * (-jnp.exp(alog_ref[...]))
    tri = _tril_ones(t)
    cs = _exact_dot(tri.astype(BF16), da)
    er = lax.broadcasted_iota(jnp.int32, (SMALL, WIDTH), 0)
    ec = lax.broadcasted_iota(jnp.int32, (SMALL, WIDTH), 1)
    expand = (er == LANE_DT + ec // SSM_HEAD_DIM).astype(BF16)
    dt_e = _exact_dot_r(dt, expand)
    cs_e = _exact_dot_r(cs, expand)
    cs_last_e = cs_e[t - 1:t, :]

    xdt = xs * dt_e
    xdt_b = xdt.astype(BF16)
    xdec_b = (xdt * jnp.exp(cs_last_e - cs_e)).astype(BF16)
    cs_t = cs.T
    lane = lax.broadcasted_iota(jnp.int32, (t, 2 * SSM_HEAD_DIM), 1)
    lo_half = lane < SSM_HEAD_DIM

    y_parts = []
    for g in range(SSM_GROUPS):
        b_g = bc[:, g * SSM_STATE:(g + 1) * SSM_STATE].astype(BF16)
        c_g = bc[:, (SSM_GROUPS + g) * SSM_STATE:(SSM_GROUPS + g + 1) * SSM_STATE].astype(BF16)
        cb = lax.dot_general(c_g, b_g, (((1,), (1,)), ((), ())), preferred_element_type=F32)
        for pair in range(SSM_HEADS // SSM_GROUPS // 2):
            h0 = g * (SSM_HEADS // SSM_GROUPS) + 2 * pair
            gs = []
            for h in (h0, h0 + 1):
                seg = cs[:, LANE_DT + h:LANE_DT + h + 1] - cs_t[LANE_DT + h:LANE_DT + h + 1, :]
                lmat = jnp.exp(jnp.where(tri, seg, -jnp.inf))
                gs.append((cb * lmat).astype(BF16))
            xp = xdt_b[:, h0 * SSM_HEAD_DIM:(h0 + 2) * SSM_HEAD_DIM]
            zero = jnp.zeros_like(xp)
            rhs = jnp.concatenate([jnp.where(lo_half, xp, zero), jnp.where(lo_half, zero, xp)], axis=0)
            y_parts.append(jnp.dot(jnp.concatenate(gs, axis=1), rhs, preferred_element_type=F32))
    y_diag = jnp.concatenate(y_parts, axis=1)

    y_off_parts = []
    for g in range(SSM_GROUPS):
        b_g = bc[:, g * SSM_STATE:(g + 1) * SSM_STATE].astype(BF16)
        c_g = bc[:, (SSM_GROUPS + g) * SSM_STATE:(SSM_GROUPS + g + 1) * SSM_STATE].astype(BF16)
        prev = state_ref[g]
        y_off_parts.append(jnp.dot(c_g, prev.astype(BF16), preferred_element_type=F32))
        upd = lax.dot_general(b_g, xdec_b[:, g * hp:(g + 1) * hp], (((0,), (0,)), ((), ())),
                              preferred_element_type=F32)
        state_ref[g] = prev * jnp.exp(cs_last_e[:, g * hp:(g + 1) * hp]) + upd
    y_off = jnp.concatenate(y_off_parts, axis=1) * jnp.exp(cs_e)

    y = y_diag + y_off + xs * dskip_ref[...]
    v = y * _silu(z_ref[...].astype(F32))
    v = v * lax.rsqrt(jnp.mean(v * v, axis=-1, keepdims=True) + EPS)
    o_ref[...] = (v * nw_ref[...]).astype(o_ref.dtype)


def _ssd(u, small, p, bsz, s_len, t):
    nt = s_len // t
    row = lambda b, i: b * nt + i
    full = lambda shape: pl.BlockSpec(shape, lambda b, i: (0, 0))
    return pl.pallas_call(
        _ssd_kernel,
        grid=(bsz, nt),
        in_specs=[pl.BlockSpec((t, WIDTH), lambda b, i: (row(b, i), COL_XS)),
                  pl.BlockSpec((t, 512), lambda b, i: (row(b, i), COL_BC_512)),
                  pl.BlockSpec((t, WIDTH), lambda b, i: (row(b, i), COL_Z)),
                  pl.BlockSpec((t, SMALL), lambda b, i: (row(b, i), 0)),
                  full((SSM_CONV, WIDTH)), full((1, WIDTH)),
                  full((SSM_CONV, 512)), full((1, 512)),
                  full((1, SMALL)), full((1, SMALL)), full((1, WIDTH)), full((1, WIDTH))],
        out_specs=pl.BlockSpec((t, WIDTH), lambda b, i: (row(b, i), 0)),
        out_shape=jax.ShapeDtypeStruct((bsz * s_len, WIDTH), BF16),
        scratch_shapes=[pltpu.VMEM((HALO + t, WIDTH), F32),
                        pltpu.VMEM((HALO + t, 512), F32),
                        pltpu.VMEM((SSM_GROUPS, SSM_STATE, 512), F32)],
        compiler_params=_cparams(("parallel", "arbitrary")),
        name="ssd",
    )(u, u, u, small, p["cwx"], p["cbx"], p["cwbc"], p["cbbc"],
      p["dtb"], p["alog"], p["dskip"], p["ssm_nw"])


def _shortconv_kernel(b_ref, c_ref, x_ref, g_ref, w_ref, cb_ref, o_ref, ext_ref):
    t = x_ref.shape[0]

    @pl.when(pl.program_id(1) == 0)
    def _():
        ext_ref[0:HALO, :] = jnp.zeros((HALO, WIDTH), F32)

    ext_ref[HALO:HALO + t, :] = c_ref[...].astype(F32) * x_ref[...].astype(F32)
    conv = _causal_conv(ext_ref, w_ref, cb_ref, SC_CONV, HALO, t)
    ext_ref[0:HALO, :] = ext_ref[t:t + HALO, :]
    o_ref[...] = (b_ref[...].astype(F32) * conv * _silu(g_ref[...].astype(F32))).astype(o_ref.dtype)


def _shortconv(u, p, bsz, s_len, t):
    nt = s_len // t
    row = lambda b, i: b * nt + i
    col = lambda c: pl.BlockSpec((t, WIDTH), lambda b, i: (row(b, i), c))
    full = lambda shape: pl.BlockSpec(shape, lambda b, i: (0, 0))
    return pl.pallas_call(
        _shortconv_kernel,
        grid=(bsz, nt),
        in_specs=[col(COL_SCB), col(COL_SCC), col(COL_SCX), col(COL_GC),
                  full((SC_CONV, WIDTH)), full((1, WIDTH))],
        out_specs=pl.BlockSpec((t, WIDTH), lambda b, i: (row(b, i), 0)),
        out_shape=jax.ShapeDtypeStruct((bsz * s_len, WIDTH), BF16),
        scratch_shapes=[pltpu.VMEM((HALO + t, WIDTH), F32)],
        compiler_params=_cparams(("parallel", "arbitrary")),
        name="shortconv",
    )(u, u, u, u, p["sc_w"], p["sc_b"])


def _conformer_kernel(a_ref, gg_ref, g_ref, w_ref, cb_ref, lnw_ref, lnb_ref, o_ref, ext_ref, shift_ref):
    t = a_ref.shape[0]

    @pl.when(pl.program_id(1) == 0)
    def _():
        ext_ref[0:CF_HALO, :] = jnp.zeros((CF_HALO, WIDTH), F32)

    ext_ref[CF_HALO:CF_HALO + t, :] = a_ref[...].astype(F32) * _sigmoid(gg_ref[...].astype(F32))
    cf = _causal_conv_long(ext_ref, shift_ref, w_ref, cb_ref, CF_CONV, CF_HALO, t)
    ext_ref[0:CF_HALO, :] = ext_ref[t:t + CF_HALO, :]
    mu = jnp.mean(cf, axis=-1, keepdims=True)
    d = cf - mu
    var = jnp.mean(d * d, axis=-1, keepdims=True)
    ln = d * lax.rsqrt(var + EPS) * lnw_ref[...] + lnb_ref[...]
    o_ref[...] = (_silu(ln) * _silu(g_ref[...].astype(F32))).astype(o_ref.dtype)


def _conformer(u, p, bsz, s_len, t):
    nt = s_len // t
    row = lambda b, i: b * nt + i
    col = lambda c: pl.BlockSpec((t, WIDTH), lambda b, i: (row(b, i), c))
    full = lambda shape: pl.BlockSpec(shape, lambda b, i: (0, 0))
    return pl.pallas_call(
        _conformer_kernel,
        grid=(bsz, nt),
        in_specs=[col(COL_GLUA), col(COL_GLUG), col(COL_GD),
                  full((CF_CONV, WIDTH)), full((1, WIDTH)), full((1, WIDTH)), full((1, WIDTH))],
        out_specs=pl.BlockSpec((t, WIDTH), lambda b, i: (row(b, i), 0)),
        out_shape=jax.ShapeDtypeStruct((bsz * s_len, WIDTH), BF16),
        scratch_shapes=[pltpu.VMEM((CF_HALO + t, WIDTH), F32), pltpu.VMEM((CF_HALO + t, WIDTH), F32)],
        compiler_params=_cparams(("parallel", "arbitrary")),
        name="conformer",
    )(u, u, u, p["cf_w"], p["cf_b"], p["cf_lnw"], p["cf_lnb"])


def _merge_kernel(h_ref, ya_ref, yb_ref, yc_ref, yd_ref, wg_ref, bg_ref, wb_ref, o_ref):
    h = h_ref[...]
    acc = None
    for i, y_ref in enumerate((ya_ref, yb_ref, yc_ref, yd_ref)):
        gate = _sigmoid(jnp.dot(h, wg_ref[i], preferred_element_type=F32) + bg_ref[i:i + 1, :])
        term = gate * jnp.dot(y_ref[...], wb_ref[i], preferred_element_type=F32)
        acc = term if acc is None else acc + term
    o_ref[...] = acc.astype(o_ref.dtype)


def _merge(h, ys, wg, bg, wb, layer, tm, tn):
    m = h.shape[0]
    yspec = pl.BlockSpec((tm, WIDTH), lambda i, j: (i, 0))
    return pl.pallas_call(
        _merge_kernel,
        grid=(m // tm, D_MODEL // tn),
        in_specs=[pl.BlockSpec((tm, D_MODEL), lambda i, j: (i, 0)),
                  yspec, yspec, yspec, yspec,
                  pl.BlockSpec((None, N_BRANCH, D_MODEL, tn), lambda i, j: (layer, 0, 0, j)),
                  pl.BlockSpec((None, N_BRANCH, tn), lambda i, j: (layer, 0, j)),
                  pl.BlockSpec((None, N_BRANCH, WIDTH, tn), lambda i, j: (layer, 0, 0, j))],
        out_specs=pl.BlockSpec((tm, tn), lambda i, j: (i, j)),
        out_shape=jax.ShapeDtypeStruct((m, D_MODEL), BF16),
        compiler_params=_cparams(("parallel", "arbitrary")),
        name="merge",
    )(h, *ys, wg, bg, wb)


def _out_kernel(x_ref, mg_ref, w_ref, nw_ref, *out_refs, last):
    x_new = x_ref[...] + jnp.dot(mg_ref[...], w_ref[...], preferred_element_type=F32)
    normed = x_new * lax.rsqrt(jnp.mean(x_new * x_new, axis=-1, keepdims=True) + EPS) * nw_ref[...]
    if last:
        out_refs[0][...] = normed.astype(out_refs[0].dtype)
    else:
        out_refs[0][...] = x_new
        out_refs[1][...] = normed.astype(out_refs[1].dtype)


def _out_proj(x2d, merged, w_out, layer, next_norm_w, tm, last):
    m, d = x2d.shape
    row = pl.BlockSpec((tm, d), lambda i: (i, 0))
    if last:
        out_shape = [jax.ShapeDtypeStruct((m, d), F32)]
    else:
        out_shape = [jax.ShapeDtypeStruct((m, d), F32), jax.ShapeDtypeStruct((m, d), BF16)]
    return pl.pallas_call(
        functools.partial(_out_kernel, last=last),
        grid=(m // tm,),
        in_specs=[row, row,
                  pl.BlockSpec((None, d, d), lambda i: (layer, 0, 0)),
                  pl.BlockSpec((1, d), lambda i: (0, 0))],
        out_specs=[row] * len(out_shape),
        out_shape=out_shape,
        compiler_params=_cparams(("parallel",)),
        name="out_proj_last" if last else "out_proj",
    )(x2d, merged, w_out, next_norm_w.reshape(1, d))


def _pad_lanes(v, start):
    n = v.shape[-1]
    return jnp.pad(v.astype(F32), ((0, 0), (start, SMALL - start - n)))[:, None, :]


def _w_relayout_kernel(wt_ref, main_ref, small_ref):
    tc = wt_ref.shape[1]
    scale = FOX_HEAD_DIM ** -0.5 * LOG2E
    f0 = 3 * WIDTH
    xbc0 = f0 + FOX_HEADS + 2 * WIDTH
    dt0 = xbc0 + WIDTH + 512
    rest0 = dt0 + SSM_HEADS
    main_ref[0:WIDTH, :] = (wt_ref[0:WIDTH, :] * scale).astype(BF16)
    main_ref[WIDTH:f0, :] = wt_ref[WIDTH:f0, :].astype(BF16)
    main_ref[f0:f0 + 2 * WIDTH, :] = wt_ref[f0 + FOX_HEADS:xbc0, :].astype(BF16)
    main_ref[5 * WIDTH:12 * WIDTH, :] = wt_ref[rest0:rest0 + 7 * WIDTH, :].astype(BF16)
    main_ref[12 * WIDTH:N_MAIN, :] = wt_ref[xbc0:dt0, :].astype(BF16)
    pad = jnp.zeros((SMALL - FOX_HEADS - SSM_HEADS, tc), F32)
    small_ref[...] = jnp.concatenate([wt_ref[f0:f0 + FOX_HEADS, :], wt_ref[dt0:rest0, :], pad],
                                     axis=0).astype(BF16)


def _w_relayout(w_in, tc):
    wt = jnp.swapaxes(w_in, 1, 2)
    depth, n_in, d = wt.shape
    return pl.pallas_call(
        _w_relayout_kernel,
        grid=(depth, d // tc),
        in_specs=[pl.BlockSpec((None, n_in, tc), lambda l, i: (l, 0, i))],
        out_specs=[pl.BlockSpec((None, N_MAIN, tc), lambda l, i: (l, 0, i)),
                   pl.BlockSpec((None, SMALL, tc), lambda l, i: (l, 0, i))],
        out_shape=[jax.ShapeDtypeStruct((depth, N_MAIN, d), BF16),
                   jax.ShapeDtypeStruct((depth, SMALL, d), BF16)],
        compiler_params=_cparams(("parallel", "parallel")),
        name="w_relayout",
    )(wt)


def _tiles(m, s_len):
    pick = lambda want, n: min(want, n)
    return dict(t_relayout=256, tm_in=pick(1024, m), tn_in=1536, tm_norm=pick(512, m),
                t_attn=pick(512, s_len), t_ssd=pick(128, s_len), t_conv=pick(256, s_len),
                tm_merge=pick(512, m), tn_merge=512, tm_out=pick(512, m))


def kernel(x, norm_w, w_in, fg_bias, ssm_conv_w, ssm_conv_b, dt_bias, a_log, d_skip, ssm_norm_w,
           sc_conv_w, sc_conv_b, cf_conv_w, cf_conv_b, cf_ln_w, cf_ln_b, w_gate, b_gate, w_branch,
           w_out, final_norm_w):
    bsz, s_len, d = x.shape
    depth = w_in.shape[0]
    m = bsz * s_len
    tl = _tiles(m, s_len)

    w_main, w_small = _w_relayout(w_in, tl["t_relayout"])
    wg = w_gate.astype(BF16)
    wb = w_branch.astype(BF16)
    wo = w_out.astype(BF16)
    fg_row = _pad_lanes(fg_bias, LANE_F)
    dtb_row = _pad_lanes(dt_bias, LANE_DT)
    alog_row = _pad_lanes(a_log, LANE_DT)
    dskip_row = jnp.repeat(d_skip.astype(F32), SSM_HEAD_DIM, axis=-1)[:, None, :]
    row = lambda a: a.astype(F32)[:, None, :]

    x2d = x.reshape(m, d)
    h = _rmsnorm(x2d, norm_w[0], tl["tm_norm"])
    for l in range(depth):
        u, small = _in_proj(h, w_main, w_small, l, tl["tm_in"], tl["tn_in"])
        qp, kp, vt = _fox_prep(u, small, fg_row[l], bsz, s_len, tl["t_attn"])
        y_a = _flash(qp, kp, vt, u, bsz, s_len, tl["t_attn"])
        ssd_p = dict(cwx=ssm_conv_w[l, :, :WIDTH], cbx=ssm_conv_b[l, None, :WIDTH],
                     cwbc=ssm_conv_w[l, :, WIDTH:], cbbc=ssm_conv_b[l, None, WIDTH:],
                     dtb=dtb_row[l], alog=alog_row[l], dskip=dskip_row[l], ssm_nw=row(ssm_norm_w)[l])
        y_b = _ssd(u, small, ssd_p, bsz, s_len, tl["t_ssd"])
        y_c = _shortconv(u, dict(sc_w=sc_conv_w[l], sc_b=sc_conv_b[l, None, :]), bsz, s_len, tl["t_conv"])
        y_d = _conformer(u, dict(cf_w=cf_conv_w[l], cf_b=cf_conv_b[l, None, :],
                                 cf_lnw=cf_ln_w[l, None, :], cf_lnb=cf_ln_b[l, None, :]),
                         bsz, s_len, tl["t_conv"])
        merged = _merge(h, (y_a, y_b, y_c, y_d), wg, b_gate, wb, l, tl["tm_merge"], tl["tn_merge"])
        last = l == depth - 1
        nw = final_norm_w if last else norm_w[l + 1]
        outs = _out_proj(x2d, merged, wo, l, nw, tl["tm_out"], last)
        if last:
            return outs[0].reshape(bsz, s_len, d)
        x2d, h = outs
```
